```python
import math
import jax, jax.numpy as jnp
from jax import lax
import numpy as np

D_MODEL = 4096
BATCH = 4
SEQ = 2048
DEPTH = 1
DEC_BATCH = 128
DEC_SEQ = 1
PAST_LEN = 16384
PAGE_SIZE = 128

D_MIX = D_MODEL
D_SSD = D_MIX // 2
SSD_HEAD_DIM = 64
N_SSD_HEADS = D_SSD // SSD_HEAD_DIM
N_SSD_GROUPS = 4
HEADS_PER_GROUP = N_SSD_HEADS // N_SSD_GROUPS
D_STATE = 128
CONV_W = 4
CONV_CH = D_SSD + 2 * N_SSD_GROUPS * D_STATE
SSD_CHUNK = 128
DT_MIN = 0.001
DT_MAX = 0.1

D_MLA = D_MIX - D_SSD
V_HEAD_DIM = 128
N_MLA_HEADS = D_MLA // V_HEAD_DIM
QK_NOPE_DIM = 128
QK_ROPE_DIM = 64
QK_HEAD_DIM = QK_NOPE_DIM + QK_ROPE_DIM
Q_LORA = 1024
KV_LORA = 512
ROPE_THETA = 10000.0
ATTN_SCALE = QK_HEAD_DIM ** -0.5
Q_BLOCK = 128

IN_SPLITS = (D_SSD,
             D_SSD + CONV_CH,
             D_SSD + CONV_CH + N_SSD_HEADS,
             D_SSD + CONV_CH + N_SSD_HEADS + Q_LORA,
             D_SSD + CONV_CH + N_SSD_HEADS + Q_LORA + KV_LORA)
IN_COLS = D_SSD + CONV_CH + N_SSD_HEADS + Q_LORA + KV_LORA + QK_ROPE_DIM

N_EXPERT_GROUPS = 8
EXPERTS_PER_GROUP = 8
N_EXPERTS = N_EXPERT_GROUPS * EXPERTS_PER_GROUP
TOP_K = 2
D_EXPERT = D_MODEL // 4
MOE_BLOCK = 128
MOE_MIN_BLOCK = 8

DEEPNORM_ALPHA = (2 * DEPTH) ** 0.25
DEEPNORM_BETA = (8 * DEPTH) ** -0.25
NORM_EPS = 1e-5

kernel_name = 'hymba_ssd_mla_hmoe_deepnorm_step'


def layer_norm(x, g, b):
    xf = x.astype(jnp.float32)
    mu = jnp.mean(xf, -1, keepdims=True)
    var = jnp.mean(jnp.square(xf - mu), -1, keepdims=True)
    return ((xf - mu) * lax.rsqrt(var + NORM_EPS) * g + b).astype(x.dtype)


def rms_norm(x, w):
    xf = x.astype(jnp.float32)
    return (xf * lax.rsqrt(jnp.mean(xf * xf, -1, keepdims=True) + NORM_EPS) * w).astype(x.dtype)


def rotary(x, pos):
    half = x.shape[-1] // 2
    inv_freq = ROPE_THETA ** (-jnp.arange(half, dtype=jnp.float32) / half)
    ang = pos.astype(jnp.float32)[:, None] * inv_freq
    shape = (ang.shape[0],) + (1,) * (x.ndim - 3) + (half,)
    cos = jnp.cos(ang).reshape(shape)
    sin = jnp.sin(ang).reshape(shape)
    x1 = x[..., :half].astype(jnp.float32)
    x2 = x[..., half:].astype(jnp.float32)
    return jnp.concatenate([x1 * cos - x2 * sin, x2 * cos + x1 * sin], -1).astype(x.dtype)


def causal_conv(xbc, buf, conv_w, conv_b):
    t = xbc.shape[1]
    xp = jnp.concatenate([buf.astype(xbc.dtype), xbc], axis=1)
    y = sum(xp[:, k:k + t] * conv_w[k] for k in range(CONV_W)) + conv_b
    return jax.nn.silu(y), xp[:, -(CONV_W - 1):]


def shared_projections(x, conv_buf, w_in, conv_w, conv_b, dt_bias):
    b, t, _ = x.shape
    proj = jnp.einsum('btd,de->bte', x, w_in)
    z, xbc, dt_raw, cq, ckv_raw, kr_raw = jnp.split(proj, IN_SPLITS, axis=-1)
    xbc, conv_new = causal_conv(xbc, conv_buf, conv_w, conv_b)
    xs, bm, cm = jnp.split(xbc, (D_SSD, D_SSD + N_SSD_GROUPS * D_STATE), axis=-1)
    xs = xs.reshape(b, t, N_SSD_HEADS, SSD_HEAD_DIM)
    bm = bm.reshape(b, t, N_SSD_GROUPS, D_STATE)
    cm = cm.reshape(b, t, N_SSD_GROUPS, D_STATE)
    dt = jax.nn.softplus(dt_raw.astype(jnp.float32) + dt_bias)
    return z, xs, bm, cm, dt, cq, ckv_raw, kr_raw, conv_new


def ssd_chunked(xs, dt, a, bm, cm):
    b, t = xs.shape[:2]
    c, q = t // SSD_CHUNK, SSD_CHUNK
    f32 = jnp.float32
    xr = xs.astype(f32).reshape(b, c, q, N_SSD_GROUPS, HEADS_PER_GROUP, SSD_HEAD_DIM)
    dtr = dt.reshape(b, c, q, N_SSD_GROUPS, HEADS_PER_GROUP)
    br = bm.astype(f32).reshape(b, c, q, N_SSD_GROUPS, D_STATE)
    cr = cm.astype(f32).reshape(b, c, q, N_SSD_GROUPS, D_STATE)
    a_cs = jnp.cumsum(dtr * a.reshape(N_SSD_GROUPS, HEADS_PER_GROUP), axis=2)
    xdt = xr * dtr[..., None]
    seg = a_cs[:, :, :, None] - a_cs[:, :, None, :]
    causal = jnp.tril(jnp.ones((q, q), bool))[:, :, None, None]
    decay = jnp.exp(jnp.where(causal, seg, -jnp.inf))
    cb = jnp.einsum('bclgn,bcsgn->bclsg', cr, br)
    y_diag = jnp.einsum('bclsg,bclsgj,bcsgjp->bclgjp', cb, decay, xdt)
    decay_to_end = jnp.exp(a_cs[:, :, -1:] - a_cs)
    chunk_states = jnp.einsum('bcsgn,bcsgj,bcsgjp->bcgjpn', br, decay_to_end, xdt)
    chunk_decay = jnp.exp(a_cs[:, :, -1])

    def step(h, inp):
        cs, cd = inp
        return h * cd[..., None, None] + cs, h

    h0 = jnp.zeros((b, N_SSD_GROUPS, HEADS_PER_GROUP, SSD_HEAD_DIM, D_STATE), f32)
    h_final, h_prev = lax.scan(step, h0, (jnp.moveaxis(chunk_states, 1, 0), jnp.moveaxis(chunk_decay, 1, 0)))
    h_prev = jnp.moveaxis(h_prev, 0, 1)
    y_off = jnp.einsum('bclgn,bcgjpn,bclgj->bclgjp', cr, h_prev, jnp.exp(a_cs))
    y = (y_diag + y_off).reshape(b, t, N_SSD_HEADS, SSD_HEAD_DIM)
    return y, h_final.reshape(b, N_SSD_HEADS, SSD_HEAD_DIM, D_STATE)


def ssd_recurrent(xs, dt, a, bm, cm, h0):
    b, t = xs.shape[:2]
    f32 = jnp.float32
    xr = xs.astype(f32).reshape(b, t, N_SSD_GROUPS, HEADS_PER_GROUP, SSD_HEAD_DIM)
    dtr = dt.reshape(b, t, N_SSD_GROUPS, HEADS_PER_GROUP)
    ag = a.reshape(N_SSD_GROUPS, HEADS_PER_GROUP)
    h = h0.astype(f32).reshape(b, N_SSD_GROUPS, HEADS_PER_GROUP, SSD_HEAD_DIM, D_STATE)

    def step(hs, inp):
        xt, dtt, bt, ct = inp
        hs = hs * jnp.exp(dtt * ag)[..., None, None] + jnp.einsum('bgj,bgn,bgjp->bgjpn', dtt, bt, xt)
        return hs, jnp.einsum('bgn,bgjpn->bgjp', ct, hs)

    h, ys = lax.scan(step, h, (jnp.moveaxis(xr, 1, 0), jnp.moveaxis(dtr, 1, 0),
                               jnp.moveaxis(bm.astype(f32), 1, 0), jnp.moveaxis(cm.astype(f32), 1, 0)))
    y = jnp.moveaxis(ys, 0, 1).reshape(b, t, N_SSD_HEADS, SSD_HEAD_DIM)
    return y, h.reshape(b, N_SSD_HEADS, SSD_HEAD_DIM, D_STATE)


def ssd_gated_output(y, xs, z, d_skip, norm_w):
    b, t = y.shape[:2]
    y = (y + d_skip[:, None] * xs.astype(jnp.float32)).reshape(b, t, D_SSD)
    y = y * jax.nn.silu(z.astype(jnp.float32))
    yg = y.reshape(b, t, N_SSD_GROUPS, D_SSD // N_SSD_GROUPS)
    yg = yg * lax.rsqrt(jnp.mean(yg * yg, -1, keepdims=True) + NORM_EPS)
    return (yg.reshape(b, t, D_SSD) * norm_w).astype(z.dtype)


def mla_project(cq, ckv_raw, kr_raw, pos, q_norm_w, w_uq, kv_norm_w):
    q = jnp.einsum('btc,chd->bthd', rms_norm(cq, q_norm_w), w_uq)
    q_nope = q[..., :QK_NOPE_DIM]
    q_rope = rotary(q[..., QK_NOPE_DIM:], pos)
    ckv = rms_norm(ckv_raw, kv_norm_w)
    k_rope = rotary(kr_raw, pos)
    return q_nope, q_rope, ckv, k_rope


def mla_prompt_attention(q_nope, q_rope, k_nope, k_rope, v):
    b, s, h, _ = q_nope.shape
    nb = s // Q_BLOCK
    qn = jnp.moveaxis(q_nope.reshape(b, nb, Q_BLOCK, h, QK_NOPE_DIM), 1, 0)
    qr = jnp.moveaxis(q_rope.reshape(b, nb, Q_BLOCK, h, QK_ROPE_DIM), 1, 0)
    kpos = jnp.arange(s)

    def block(args):
        i, qn_b, qr_b = args
        sc = (jnp.einsum('bqhd,bkhd->bhqk', qn_b, k_nope)
              + jnp.einsum('bqhr,bkr->bhqk', qr_b, k_rope)).astype(jnp.float32) * ATTN_SCALE
        qpos = i * Q_BLOCK + jnp.arange(Q_BLOCK)
        sc = jnp.where(kpos[None, :] <= qpos[:, None], sc, -jnp.inf)
        p = jax.nn.softmax(sc, axis=-1).astype(v.dtype)
        return jnp.einsum('bhqk,bkhd->bqhd', p, v)

    o = lax.map(block, (jnp.arange(nb), qn, qr))
    return jnp.moveaxis(o, 0, 1).reshape(b, s, h, V_HEAD_DIM)


def mla_sample_attention(q_nope, q_rope, ckv_new, kr_new, cache_ckv, cache_krope, page_table, l, w_uk, w_uv):
    n_pages = page_table.shape[1]
    past = n_pages * PAGE_SIZE
    t = q_nope.shape[1]
    q_lat = jnp.einsum('bthd,chd->bthc', q_nope, w_uk)
    kpos = jnp.concatenate([jnp.arange(past), past + jnp.arange(t)])
    qpos = past + jnp.arange(t)
    mask = kpos[None, :] <= qpos[:, None]

    def one_sequence(args):
        ql, qr, pages, c_new, r_new = args
        c_all = jnp.concatenate([cache_ckv[l, pages].reshape(past, KV_LORA), c_new], 0)
        r_all = jnp.concatenate([cache_krope[l, pages].reshape(past, QK_ROPE_DIM), r_new], 0)
        sc = (jnp.einsum('thc,sc->hts', ql, c_all)
              + jnp.einsum('thr,sr->hts', qr, r_all)).astype(jnp.float32) * ATTN_SCALE
        p = jax.nn.softmax(jnp.where(mask, sc, -jnp.inf), axis=-1).astype(c_all.dtype)
        return jnp.einsum('hts,sc->thc', p, c_all)

    o_lat = lax.map(one_sequence, (q_lat, q_rope, page_table, ckv_new, kr_new))
    return jnp.einsum('bthc,chd->bthd', o_lat, w_uv)


def merge_heads(ssd_out, mla_out, w_out):
    b, t = ssd_out.shape[:2]
    cat = jnp.concatenate([ssd_out, mla_out.reshape(b, t, D_MLA).astype(ssd_out.dtype)], -1)
    return jnp.einsum('bte,ed->btd', cat, w_out)


def routed_expert_mlp(xf, experts, gates, w_gate_e, w_up_e, w_down_e, l):
    t, d = xf.shape
    n_assign = t * TOP_K
    blk = max(MOE_MIN_BLOCK, min(MOE_BLOCK, n_assign // N_EXPERTS))
    n_blocks = (n_assign + N_EXPERTS * (blk - 1) + blk - 1) // blk
    n_rows = n_blocks * blk
    e_flat = experts.reshape(-1)
    tok_flat = jnp.arange(n_assign, dtype=jnp.int32) // TOP_K
    order = jnp.argsort(e_flat)
    e_sorted = e_flat[order]
    counts = jnp.bincount(e_flat, length=N_EXPERTS)
    padded = (counts + blk - 1) // blk * blk
    pad_end = jnp.cumsum(padded)
    pad_start = pad_end - padded
    start = jnp.cumsum(counts) - counts
    dest = pad_start[e_sorted] + jnp.arange(n_assign) - start[e_sorted]
    row_tok = jnp.full((n_rows,), t, jnp.int32).at[dest].set(tok_flat[order])
    row_gate = jnp.zeros((n_rows,), gates.dtype).at[dest].set(gates.reshape(-1)[order])
    blk_expert = jnp.minimum(jnp.searchsorted(pad_end, jnp.arange(n_blocks) * blk, side='right'), N_EXPERTS - 1)
    x_pad = jnp.concatenate([xf, jnp.zeros((1, d), xf.dtype)], 0)

    def block(args):
        rows, e = args
        xb = x_pad[rows]
        hdn = jax.nn.silu(xb @ w_gate_e[l, e]) * (xb @ w_up_e[l, e])
        return hdn @ w_down_e[l, e]

    y_rows = lax.map(block, (row_tok.reshape(n_blocks, blk), blk_expert))
    y = jax.ops.segment_sum(y_rows.reshape(n_rows, d) * row_gate[:, None], row_tok, num_segments=t + 1)
    return y[:t].astype(xf.dtype)


def hierarchical_moe(x, w_rg, b_rg, w_re, b_re, w_gate_e, w_up_e, w_down_e, l):
    shp = x.shape
    xf = x.reshape(-1, shp[-1])
    t = xf.shape[0]
    g_prob = jax.nn.softmax((xf @ w_rg).astype(jnp.float32) + b_rg, axis=-1)
    g_val, g_idx = lax.top_k(g_prob, 1)
    e_logits = ((xf @ w_re).astype(jnp.float32) + b_re).reshape(t, N_EXPERT_GROUPS, EXPERTS_PER_GROUP)
    e_in = e_logits[jnp.arange(t), g_idx[:, 0]]
    e_val, e_loc = lax.top_k(jax.nn.softmax(e_in, axis=-1), TOP_K)
    e_val = e_val / jnp.sum(e_val, -1, keepdims=True)
    gates = g_val * e_val
    experts = g_idx * EXPERTS_PER_GROUP + e_loc
    return routed_expert_mlp(xf, experts, gates, w_gate_e, w_up_e, w_down_e, l).reshape(shp)


def deepnorm_moe(h, mix, l, ln1_g, ln1_b, ln2_g, ln2_b, w_rg, b_rg, w_re, b_re, w_gate_e, w_up_e, w_down_e):
    h = layer_norm(DEEPNORM_ALPHA * h + mix, ln1_g, ln1_b)
    ffn = hierarchical_moe(h, w_rg, b_rg, w_re, b_re, w_gate_e, w_up_e, w_down_e, l)
    return layer_norm(DEEPNORM_ALPHA * h + ffn, ln2_g, ln2_b)


def setup_inputs(seed: int = 0) -> dict:
    key = jax.random.key(seed)
    ks = jax.random.split(key, 32)
    f32 = jnp.float32
    n_pages = PAST_LEN // PAGE_SIZE
    n_used = DEC_BATCH * n_pages
    n_pool = n_used + n_used // 4

    def normal(k, shape, scale):
        return jax.random.normal(k, shape, f32) * scale

    dt0 = jnp.exp(jax.random.uniform(ks[10], (DEPTH, N_SSD_HEADS), f32, math.log(DT_MIN), math.log(DT_MAX)))
    page_table = jax.random.permutation(ks[6], n_pool)[:n_used].reshape(DEC_BATCH, n_pages).astype(jnp.int32)
    return {
        'x_prompt': normal(ks[0], (BATCH, SEQ, D_MODEL), 1.0),
        'x_sample': normal(ks[1], (DEC_BATCH, DEC_SEQ, D_MODEL), 1.0),
        'cache_ckv': normal(ks[2], (DEPTH, n_pool, PAGE_SIZE, KV_LORA), 1.0),
        'cache_krope': normal(ks[3], (DEPTH, n_pool, PAGE_SIZE, QK_ROPE_DIM), 1.0),
        'state_ssm': normal(ks[4], (DEPTH, DEC_BATCH, N_SSD_HEADS, SSD_HEAD_DIM, D_STATE), 0.1),
        'state_conv': normal(ks[5], (DEPTH, DEC_BATCH, CONV_W - 1, CONV_CH), 1.0),
        'page_table': page_table,
        'w_in': normal(ks[7], (DEPTH, D_MODEL, IN_COLS), D_MODEL ** -0.5),
        'conv_w': normal(ks[8], (DEPTH, CONV_W, CONV_CH), CONV_W ** -0.5),
        'conv_b': normal(ks[9], (DEPTH, CONV_CH), 0.01),
        'dt_bias': dt0 + jnp.log(-jnp.expm1(-dt0)),
        'a_log': jnp.log(jax.random.uniform(ks[11], (DEPTH, N_SSD_HEADS), f32, 1.0, 16.0)),
        'd_skip': 1.0 + normal(ks[12], (DEPTH, N_SSD_HEADS), 0.01),
        'ssd_norm_w': 1.0 + normal(ks[13], (DEPTH, D_SSD), 0.01),
        'q_norm_w': 1.0 + normal(ks[14], (DEPTH, Q_LORA), 0.01),
        'w_uq': normal(ks[15], (DEPTH, Q_LORA, N_MLA_HEADS, QK_HEAD_DIM), Q_LORA ** -0.5),
        'kv_norm_w': 1.0 + normal(ks[16], (DEPTH, KV_LORA), 0.01),
        'w_uk': normal(ks[17], (DEPTH, KV_LORA, N_MLA_HEADS, QK_NOPE_DIM), KV_LORA ** -0.5),
        'w_uv': normal(ks[18], (DEPTH, KV_LORA, N_MLA_HEADS, V_HEAD_DIM), KV_LORA ** -0.5),
        'w_out': normal(ks[19], (DEPTH, D_MIX, D_MODEL), DEEPNORM_BETA * D_MIX ** -0.5),
        'ln1_g': 1.0 + normal(ks[20], (DEPTH, D_MODEL), 0.01),
        'ln1_b': normal(ks[21], (DEPTH, D_MODEL), 0.01),
        'w_router_group': normal(ks[22], (DEPTH, D_MODEL, N_EXPERT_GROUPS), D_MODEL ** -0.5),
        'b_router_group': normal(ks[23], (DEPTH, N_EXPERT_GROUPS), 0.01),
        'w_router_expert': normal(ks[24], (DEPTH, D_MODEL, N_EXPERTS), D_MODEL ** -0.5),
        'b_router_expert': normal(ks[25], (DEPTH, N_EXPERTS), 0.01),
        'w_gate_e': normal(ks[26], (DEPTH, N_EXPERTS, D_MODEL, D_EXPERT), D_MODEL ** -0.5),
        'w_up_e': normal(ks[27], (DEPTH, N_EXPERTS, D_MODEL, D_EXPERT), D_MODEL ** -0.5),
        'w_down_e': normal(ks[28], (DEPTH, N_EXPERTS, D_EXPERT, D_MODEL), DEEPNORM_BETA * D_EXPERT ** -0.5),
        'ln2_g': 1.0 + normal(ks[29], (DEPTH, D_MODEL), 0.01),
        'ln2_b': normal(ks[30], (DEPTH, D_MODEL), 0.01),
    }


def reference(x_prompt, x_sample, cache_ckv, cache_krope, state_ssm, state_conv, page_table,
              w_in, conv_w, conv_b, dt_bias, a_log, d_skip, ssd_norm_w, q_norm_w, w_uq, kv_norm_w,
              w_uk, w_uv, w_out, ln1_g, ln1_b, w_router_group, b_router_group, w_router_expert,
              b_router_expert, w_gate_e, w_up_e, w_down_e, ln2_g, ln2_b):
    past = page_table.shape[1] * PAGE_SIZE
    pos_p = jnp.arange(x_prompt.shape[1])
    pos_s = past + jnp.arange(x_sample.shape[1])
    hp, hs = x_prompt, x_sample
    ckv_p, kr_p, ssm_p, conv_p = [], [], [], []
    ckv_s, kr_s, ssm_s, conv_s = [], [], [], []
    for l in range(DEPTH):
        a_l = -jnp.exp(a_log[l].astype(jnp.float32))
        ffn_w = (ln1_g[l], ln1_b[l], ln2_g[l], ln2_b[l], w_router_group[l], b_router_group[l],
                 w_router_expert[l], b_router_expert[l], w_gate_e, w_up_e, w_down_e)

        zero_buf = jnp.zeros((hp.shape[0], CONV_W - 1, CONV_CH), hp.dtype)
        z, xs, bm, cm, dt, cq, ckv_raw, kr_raw, conv_new = shared_projections(
            hp, zero_buf, w_in[l], conv_w[l], conv_b[l], dt_bias[l])
        y, ssm_new = ssd_chunked(xs, dt, a_l, bm, cm)
        ssd_out = ssd_gated_output(y, xs, z, d_skip[l], ssd_norm_w[l])
        q_nope, q_rope, ckv, k_rope = mla_project(cq, ckv_raw, kr_raw, pos_p, q_norm_w[l], w_uq[l], kv_norm_w[l])
        k_nope = jnp.einsum('btc,chd->bthd', ckv, w_uk[l])
        v = jnp.einsum('btc,chd->bthd', ckv, w_uv[l])
        mla_out = mla_prompt_attention(q_nope, q_rope, k_nope, k_rope, v)
        hp = deepnorm_moe(hp, merge_heads(ssd_out, mla_out, w_out[l]), l, *ffn_w)
        ckv_p.append(ckv.astype(cache_ckv.dtype))
        kr_p.append(k_rope.astype(cache_krope.dtype))
        ssm_p.append(ssm_new.astype(state_ssm.dtype))
        conv_p.append(conv_new.astype(state_conv.dtype))

        z, xs, bm, cm, dt, cq, ckv_raw, kr_raw, conv_new = shared_projections(
            hs, state_conv[l], w_in[l], conv_w[l], conv_b[l], dt_bias[l])
        y, ssm_new = ssd_recurrent(xs, dt, a_l, bm, cm, state_ssm[l])
        ssd_out = ssd_gated_output(y, xs, z, d_skip[l], ssd_norm_w[l])
        q_nope, q_rope, ckv, k_rope = mla_project(cq, ckv_raw, kr_raw, pos_s, q_norm_w[l], w_uq[l], kv_norm_w[l])
        ckv = ckv.astype(cache_ckv.dtype)
        k_rope = k_rope.astype(cache_krope.dtype)
        mla_out = mla_sample_attention(q_nope, q_rope, ckv, k_rope, cache_ckv, cache_krope,
                                       page_table, l, w_uk[l], w_uv[l])
        hs = deepnorm_moe(hs, merge_heads(ssd_out, mla_out, w_out[l]), l, *ffn_w)
        ckv_s.append(ckv)
        kr_s.append(k_rope)
        ssm_s.append(ssm_new.astype(state_ssm.dtype))
        conv_s.append(conv_new.astype(state_conv.dtype))

    return (hp, hs,
            jnp.stack(ckv_p), jnp.stack(kr_p), jnp.stack(ssm_p), jnp.stack(conv_p),
            jnp.stack(ckv_s), jnp.stack(kr_s), jnp.stack(ssm_s), jnp.stack(conv_s))
```

```python
import functools
import math

import jax
import jax.numpy as jnp
from jax import lax
from jax.experimental import pallas as pl
from jax.experimental.pallas import tpu as pltpu

F32 = jnp.float32
BF16 = jnp.bfloat16

D_MODEL = 4096
D_SSD = 2048
SSD_HEAD_DIM = 64
N_SSD_HEADS = 32
N_SSD_GROUPS = 4
GROUP_W = D_SSD // N_SSD_GROUPS
D_STATE = 128
CONV_W = 4
CONV_CH = D_SSD + 2 * N_SSD_GROUPS * D_STATE
SSD_CHUNK = 128
N_MLA_HEADS = 16
V_HEAD_DIM = 128
QK_NOPE_DIM = 128
QK_ROPE_DIM = 64
Q_LORA = 1024
KV_LORA = 512
ROPE_THETA = 10000.0
ATTN_SCALE = (QK_NOPE_DIM + QK_ROPE_DIM) ** -0.5
PAGE_SIZE = 128
N_EXPERT_GROUPS = 8
EXPERTS_PER_GROUP = 8
N_EXPERTS = 64
TOP_K = 2
D_EXPERT = 1024
DEPTH = 1
DEEPNORM_ALPHA = (2 * DEPTH) ** 0.25
NORM_EPS = 1e-5

LANE = 128
SUBLANE = 8
QK_PAD = 256
QLAT_W = 640

C_XBC, C_CQ, C_Z, C_CKV, C_KR, C_DT = 0, 3072, 4096, 6144, 6656, 6784
PROJ_W = 6912

MOE_BLK = 512
MOE_SUB = 128
MOE_TN = 256
NEG_BIG = -1e30


def _cparams(sem, vmem_mb):
    return pltpu.CompilerParams(dimension_semantics=sem, vmem_limit_bytes=vmem_mb * 1024 * 1024)


def _sigmoid(x):
    return 1.0 / (1.0 + jnp.exp(-x))


def _silu(x):
    return x * _sigmoid(x)


def _split3(x):
    hi = x.astype(BF16)
    r1 = x - hi.astype(F32)
    mid = r1.astype(BF16)
    lo = (r1 - mid.astype(F32)).astype(BF16)
    return hi, mid, lo


def _expand(x, e):
    hi, mid, lo = _split3(x)
    d = functools.partial(jnp.dot, preferred_element_type=F32)
    return d(hi, e) + d(mid, e) + d(lo, e)


def _dot_nt(a, b):
    return lax.dot_general(a, b, (((1,), (1,)), ((), ())), preferred_element_type=F32)


def _dot_tn(a, b):
    return lax.dot_general(a, b, (((0,), (0,)), ((), ())), preferred_element_type=F32)


def _mm_kernel(x_ref, w_ref, o_ref, xb_ref):
    @pl.when(pl.program_id(1) == 0)
    def _():
        xb_ref[...] = x_ref[...].astype(BF16)

    o_ref[...] = jnp.dot(xb_ref[...], w_ref[...], preferred_element_type=F32)


def _matmul(x, w, tm, tn):
    m, k = x.shape
    n = w.shape[1]
    return pl.pallas_call(
        _mm_kernel,
        grid=(m // tm, n // tn),
        in_specs=[pl.BlockSpec((tm, k), lambda i, j: (i, 0)),
                  pl.BlockSpec((k, tn), lambda i, j: (0, j))],
        out_specs=pl.BlockSpec((tm, tn), lambda i, j: (i, j)),
        out_shape=jax.ShapeDtypeStruct((m, n), F32),
        scratch_shapes=[pltpu.VMEM((tm, k), BF16)],
        compiler_params=_cparams(("parallel", "arbitrary"), 48),
        name="in_proj",
    )(x, w)


def _softplus(x):
    return jnp.maximum(x, 0.0) + jnp.log1p(jnp.exp(-jnp.abs(x)))


def _gated_norm(y, xs, z, dsk, nw):
    y = (y + dsk * xs) * _silu(z)
    outs = []
    for g in range(N_SSD_GROUPS):
        yg = y[:, g * GROUP_W:(g + 1) * GROUP_W]
        ms = jnp.mean(yg * yg, axis=1, keepdims=True)
        outs.append(yg * lax.rsqrt(ms + NORM_EPS))
    return jnp.concatenate(outs, axis=1) * nw


def _cumsum_rows(x):
    row = lax.broadcasted_iota(jnp.int32, x.shape, 0)
    s = 1
    while s < x.shape[0]:
        x = x + jnp.where(row >= s, pltpu.roll(x, s, 0), 0.0)
        s *= 2
    return x


def _ssd_prompt_kernel(xbc_ref, z_ref, dt_ref, cw_ref, cb_ref, dtb_ref, a_ref, dsk_ref, nw_ref,
                       e64_ref, e128_ref, out_ref, st_ref, conv_ref, xs_scr, st_scr):
    c = pl.program_id(1)
    q = SSD_CHUNK

    @pl.when(c == 0)
    def _():
        xs_scr[0:SUBLANE, :] = jnp.zeros((SUBLANE, CONV_CH), F32)
        st_scr[...] = jnp.zeros_like(st_scr)

    xs_scr[SUBLANE:SUBLANE + q, :] = xbc_ref[...]
    acc = cb_ref[...] + cw_ref[CONV_W - 1:CONV_W, :] * xs_scr[SUBLANE:SUBLANE + q, :]
    for k in range(CONV_W - 1):
        off = SUBLANE - (CONV_W - 1) + k
        acc = acc + cw_ref[k:k + 1, :] * xs_scr[off:off + q, :]
    xbc = _silu(acc)
    conv_ref[0] = xs_scr[q + SUBLANE - (CONV_W - 1):q + SUBLANE, :]
    xs_scr[0:SUBLANE, :] = xs_scr[q:q + SUBLANE, :]

    xs = xbc[:, :D_SSD]
    bm = xbc[:, D_SSD:D_SSD + N_SSD_GROUPS * D_STATE]
    cm = xbc[:, D_SSD + N_SSD_GROUPS * D_STATE:]

    lane = lax.broadcasted_iota(jnp.int32, (q, LANE), 1)
    row = lax.broadcasted_iota(jnp.int32, (q, LANE), 0)
    dt = jnp.where(lane < N_SSD_HEADS, _softplus(dt_ref[...] + dtb_ref[...]), 0.0)
    a_cs = _cumsum_rows(dt * a_ref[...])
    a_cs_t = a_cs.T
    e64 = e64_ref[...]
    dt_e = _expand(dt, e64)
    acs_e = _expand(a_cs, e64)
    acs_b = _expand(a_cs, e128_ref[...])
    xdt = xs * dt_e
    last = acs_e[q - 1:q, :]
    xw = (xdt * jnp.exp(last - acs_e)).astype(BF16)
    xdt_b = xdt.astype(BF16)
    chunk_decay = jnp.exp(last)
    in_decay = jnp.exp(acs_e)
    causal = row >= lane

    ys = []
    for g in range(N_SSD_GROUPS):
        gs = slice(g * GROUP_W, (g + 1) * GROUP_W)
        bg = bm[:, g * D_STATE:(g + 1) * D_STATE].astype(BF16)
        cg = cm[:, g * D_STATE:(g + 1) * D_STATE].astype(BF16)
        cb = _dot_nt(cg, bg)
        st_g = st_scr[:, gs]
        y_off = jnp.dot(cg, st_g.astype(BF16), preferred_element_type=F32) * in_decay[:, gs]
        yd = []
        for pr in range(GROUP_W // LANE):
            h0 = g * (N_SSD_HEADS // N_SSD_GROUPS) + 2 * pr
            ms = []
            for h in (h0, h0 + 1):
                seg = acs_b[:, h * LANE:(h + 1) * LANE] - a_cs_t[h:h + 1, :]
                dec = jnp.exp(jnp.where(causal, seg, -jnp.inf))
                ms.append((cb * dec).astype(BF16))
            lhs = jnp.concatenate(ms, axis=1)
            xp = xdt_b[:, h0 * SSD_HEAD_DIM:(h0 + 2) * SSD_HEAD_DIM]
            zero = jnp.zeros_like(xp)
            rhs = jnp.concatenate([jnp.where(lane < SSD_HEAD_DIM, xp, zero),
                                   jnp.where(lane >= SSD_HEAD_DIM, xp, zero)], axis=0)
            yd.append(jnp.dot(lhs, rhs, preferred_element_type=F32))
        ys.append(jnp.concatenate(yd, axis=1) + y_off)
        st_scr[:, gs] = st_g * chunk_decay[:, gs] + _dot_tn(bg, xw[:, gs])
    y = jnp.concatenate(ys, axis=1)

    out_ref[...] = _gated_norm(y, xs, z_ref[...], dsk_ref[...], nw_ref[...]).astype(BF16)

    @pl.when(c == pl.num_programs(1) - 1)
    def _():
        st_ref[0] = st_scr[...].T.reshape(N_SSD_HEADS, SSD_HEAD_DIM, D_STATE)


def _ssd_prompt(proj, b, t, cw, cb, dtb, a, dsk, nw, e64, e128):
    nc = t // SSD_CHUNK
    q = SSD_CHUNK
    const = lambda shape: pl.BlockSpec(shape, lambda i, j: (0, 0))
    return pl.pallas_call(
        _ssd_prompt_kernel,
        grid=(b, nc),
        in_specs=[pl.BlockSpec((q, CONV_CH), lambda i, j: (i * nc + j, C_XBC // CONV_CH)),
                  pl.BlockSpec((q, D_SSD), lambda i, j: (i * nc + j, C_Z // D_SSD)),
                  pl.BlockSpec((q, LANE), lambda i, j: (i * nc + j, C_DT // LANE)),
                  const((CONV_W, CONV_CH)), const((1, CONV_CH)), const((1, LANE)), const((1, LANE)),
                  const((1, D_SSD)), const((1, D_SSD)), const((LANE, D_SSD)),
                  const((LANE, N_SSD_HEADS * LANE))],
        out_specs=[pl.BlockSpec((q, D_SSD), lambda i, j: (i * nc + j, 0)),
                   pl.BlockSpec((1, N_SSD_HEADS, SSD_HEAD_DIM, D_STATE), lambda i, j: (i, 0, 0, 0)),
                   pl.BlockSpec((1, CONV_W - 1, CONV_CH), lambda i, j: (i, 0, 0))],
        out_shape=[jax.ShapeDtypeStruct((b * t, D_SSD), BF16),
                   jax.ShapeDtypeStruct((b, N_SSD_HEADS, SSD_HEAD_DIM, D_STATE), F32),
                   jax.ShapeDtypeStruct((b, CONV_W - 1, CONV_CH), F32)],
        scratch_shapes=[pltpu.VMEM((q + SUBLANE, CONV_CH), F32), pltpu.VMEM((D_STATE, D_SSD), F32)],
        compiler_params=_cparams(("parallel", "arbitrary"), 48),
        name="ssd_prompt",
    )(proj, proj, proj, cw, cb, dtb, a, dsk, nw, e64, e128)


def _ssd_sample_kernel(xbc_ref, z_ref, dt_ref, cbuf_ref, st_in_ref, cw_ref, cb_ref, dtb_ref, a_ref,
                       dsk_ref, nw_ref, e64_ref, out_ref, st_out_ref, conv_ref,
                       xdt_t, dec_t, b_scr, c_scr, xs_scr, y_scr):
    s = pl.program_id(0)
    n = pl.num_programs(0)
    rows = N_SSD_HEADS * SSD_HEAD_DIM

    @pl.when(s == 0)
    def _():
        x_new = xbc_ref[...]
        acc = cb_ref[...] + cw_ref[CONV_W - 1:CONV_W, :] * x_new
        for k in range(CONV_W - 1):
            acc = acc + cw_ref[k:k + 1, :] * cbuf_ref[k]
        xbc = _silu(acc)
        for k in range(CONV_W - 2):
            conv_ref[k] = cbuf_ref[k + 1]
        conv_ref[CONV_W - 2] = x_new
        xs = xbc[:, :D_SSD]
        lane = lax.broadcasted_iota(jnp.int32, dt_ref.shape, 1)
        dt = jnp.where(lane < N_SSD_HEADS, _softplus(dt_ref[...] + dtb_ref[...]), 0.0)
        e64 = e64_ref[...]
        dt_e = _expand(dt, e64)
        da_e = _expand(dt * a_ref[...], e64)
        xs_scr[...] = xs
        xdt_t[...] = (xs * dt_e).T
        dec_t[...] = jnp.exp(da_e).T
        b_scr[...] = xbc[:, D_SSD:D_SSD + N_SSD_GROUPS * D_STATE]
        c_scr[...] = xbc[:, D_SSD + N_SSD_GROUPS * D_STATE:]

    st = st_in_ref[0].reshape(rows, D_STATE)
    lane = lax.broadcasted_iota(jnp.int32, (rows, xdt_t.shape[1]), 1)
    pick = lane == s
    xcol = jnp.sum(jnp.where(pick, xdt_t[...], 0.0), axis=1, keepdims=True)
    dcol = jnp.sum(jnp.where(pick, dec_t[...], 0.0), axis=1, keepdims=True)
    brow = b_scr[pl.ds(s, 1), :]
    crow = c_scr[pl.ds(s, 1), :]
    bfull = jnp.concatenate(
        [jnp.broadcast_to(brow[:, g * D_STATE:(g + 1) * D_STATE], (GROUP_W, D_STATE))
         for g in range(N_SSD_GROUPS)], axis=0)
    st_new = st * dcol + xcol * bfull
    st_out_ref[0] = st_new.reshape(N_SSD_HEADS, SSD_HEAD_DIM, D_STATE)
    st_b = st_new.astype(BF16)
    ys = []
    for g in range(N_SSD_GROUPS):
        cg = jnp.broadcast_to(crow[:, g * D_STATE:(g + 1) * D_STATE], (2 * SUBLANE, D_STATE)).astype(BF16)
        yg = _dot_nt(cg, st_b[g * GROUP_W:(g + 1) * GROUP_W, :])
        ys.append(yg[0:1, :])
    y_scr[pl.ds(s, 1), :] = jnp.concatenate(ys, axis=1)

    @pl.when(s == n - 1)
    def _():
        out_ref[...] = _gated_norm(y_scr[...], xs_scr[...], z_ref[...], dsk_ref[...],
                                   nw_ref[...]).astype(BF16)


def _ssd_sample(proj, conv_buf_t, state, cw, cb, dtb, a, dsk, nw, e64):
    nb = proj.shape[0]
    const2 = lambda shape: pl.BlockSpec(shape, lambda i: (0, 0))
    return pl.pallas_call(
        _ssd_sample_kernel,
        grid=(nb,),
        in_specs=[pl.BlockSpec((nb, CONV_CH), lambda i: (0, C_XBC // CONV_CH)),
                  pl.BlockSpec((nb, D_SSD), lambda i: (0, C_Z // D_SSD)),
                  pl.BlockSpec((nb, LANE), lambda i: (0, C_DT // LANE)),
                  pl.BlockSpec((CONV_W - 1, nb, CONV_CH), lambda i: (0, 0, 0)),
                  pl.BlockSpec((1, N_SSD_HEADS, SSD_HEAD_DIM, D_STATE), lambda i: (i, 0, 0, 0)),
                  const2((CONV_W, CONV_CH)), const2((1, CONV_CH)), const2((1, LANE)), const2((1, LANE)),
                  const2((1, D_SSD)), const2((1, D_SSD)), const2((LANE, D_SSD))],
        out_specs=[pl.BlockSpec((nb, D_SSD), lambda i: (0, 0)),
                   pl.BlockSpec((1, N_SSD_HEADS, SSD_HEAD_DIM, D_STATE), lambda i: (i, 0, 0, 0)),
                   pl.BlockSpec((CONV_W - 1, nb, CONV_CH), lambda i: (0, 0, 0))],
        out_shape=[jax.ShapeDtypeStruct((nb, D_SSD), BF16),
                   jax.ShapeDtypeStruct(state.shape, F32),
                   jax.ShapeDtypeStruct((CONV_W - 1, nb, CONV_CH), F32)],
        scratch_shapes=[pltpu.VMEM((D_SSD, nb), F32), pltpu.VMEM((D_SSD, nb), F32),
                        pltpu.VMEM((nb, N_SSD_GROUPS * D_STATE), F32),
                        pltpu.VMEM((nb, N_SSD_GROUPS * D_STATE), F32),
                        pltpu.VMEM((nb, D_SSD), F32), pltpu.VMEM((nb, D_SSD), F32)],
        compiler_params=_cparams(("arbitrary",), 48),
        name="ssd_sample",
    )(proj, proj, proj, conv_buf_t, state, cw, cb, dtb, a, dsk, nw, e64)


def _rms(x, w):
    return x * lax.rsqrt(jnp.mean(x * x, axis=1, keepdims=True) + NORM_EPS) * w


def _rope_fold(t):
    lane = lax.broadcasted_iota(jnp.int32, t.shape, 1)
    return jnp.where(lane < QK_ROPE_DIM, t + pltpu.roll(t, QK_ROPE_DIM, 1), 0.0)


def _mla_prep_kernel(cq_ref, ckv_ref, kr_ref, tab_ref, qnw_ref, kvnw_ref, wq_ref, wkv_ref,
                     q_ref, ckvn_ref, krope_ref, *kv_refs, with_kv):
    tab = tab_ref[...]
    qn = _rms(cq_ref[...], qnw_ref[...]).astype(BF16)
    qf = jnp.dot(qn, wq_ref[...], preferred_element_type=F32)
    for h in range(N_MLA_HEADS):
        base = h * QK_PAD
        q_ref[:, base:base + QK_NOPE_DIM] = qf[:, base:base + QK_NOPE_DIM].astype(BF16)
        q_ref[:, base + QK_NOPE_DIM:base + QK_PAD] = _rope_fold(
            qf[:, base + QK_NOPE_DIM:base + QK_PAD] * tab).astype(BF16)
    ckvn = _rms(ckv_ref[...], kvnw_ref[...])
    ckvn_ref[...] = ckvn
    kro = _rope_fold(kr_ref[...] * tab)
    krope_ref[...] = kro[:, :QK_ROPE_DIM]
    if with_kv:
        k_ref, v_ref = kv_refs
        kv = jnp.dot(ckvn.astype(BF16), wkv_ref[...], preferred_element_type=F32)
        kro_b = kro.astype(BF16)
        for h in range(N_MLA_HEADS):
            k_ref[:, h * QK_PAD:h * QK_PAD + QK_NOPE_DIM] = kv[:, h * LANE:(h + 1) * LANE].astype(BF16)
            k_ref[:, h * QK_PAD + QK_NOPE_DIM:(h + 1) * QK_PAD] = kro_b
        v_ref[...] = kv[:, N_MLA_HEADS * LANE:].astype(BF16)


def _mla_prep(proj, tab, qnw, kvnw, wq, wkv, tm, with_kv):
    m = proj.shape[0]
    npos = tab.shape[0] // tm
    const = lambda shape: pl.BlockSpec(shape, lambda i: (0, 0))
    out_specs = [pl.BlockSpec((tm, N_MLA_HEADS * QK_PAD), lambda i: (i, 0)),
                 pl.BlockSpec((tm, KV_LORA), lambda i: (i, 0)),
                 pl.BlockSpec((tm, QK_ROPE_DIM), lambda i: (i, 0))]
    out_shape = [jax.ShapeDtypeStruct((m, N_MLA_HEADS * QK_PAD), BF16),
                 jax.ShapeDtypeStruct((m, KV_LORA), F32),
                 jax.ShapeDtypeStruct((m, QK_ROPE_DIM), F32)]
    if with_kv:
        out_specs += [pl.BlockSpec((tm, N_MLA_HEADS * QK_PAD), lambda i: (i, 0)),
                      pl.BlockSpec((tm, N_MLA_HEADS * V_HEAD_DIM), lambda i: (i, 0))]
        out_shape += [jax.ShapeDtypeStruct((m, N_MLA_HEADS * QK_PAD), BF16),
                      jax.ShapeDtypeStruct((m, N_MLA_HEADS * V_HEAD_DIM), BF16)]
    return pl.pallas_call(
        functools.partial(_mla_prep_kernel, with_kv=with_kv),
        grid=(m // tm,),
        in_specs=[pl.BlockSpec((tm, Q_LORA), lambda i: (i, C_CQ // Q_LORA)),
                  pl.BlockSpec((tm, KV_LORA), lambda i: (i, C_CKV // KV_LORA)),
                  pl.BlockSpec((tm, LANE), lambda i: (i, C_KR // LANE)),
                  pl.BlockSpec((tm, LANE), lambda i: (i % npos, 0)),
                  const((1, Q_LORA)), const((1, KV_LORA)),
                  const((Q_LORA, N_MLA_HEADS * QK_PAD)),
                  const((KV_LORA, 2 * N_MLA_HEADS * V_HEAD_DIM))],
        out_specs=out_specs,
        out_shape=out_shape,
        compiler_params=_cparams(("parallel",), 56),
        name="mla_prep",
    )(proj, proj, proj, tab, qnw, kvnw, wq, wkv)


def _flash_kernel(q_ref, k_ref, v_ref, o_ref, *, tq):
    qi = pl.program_id(2)
    q = q_ref[...]
    row = qi * tq + lax.broadcasted_iota(jnp.int32, (tq, tq), 0)
    col0 = lax.broadcasted_iota(jnp.int32, (tq, tq), 1)

    def body(j, carry):
        m, l, acc = carry
        off = pl.multiple_of(j * tq, tq)
        k = k_ref[pl.ds(off, tq), :]
        v = v_ref[pl.ds(off, tq), :]
        s = _dot_nt(q, k) * ATTN_SCALE
        s = jnp.where(col0 + j * tq <= row, s, NEG_BIG)
        m_new = jnp.maximum(m, jnp.max(s, axis=1, keepdims=True))
        alpha = jnp.exp(m - m_new)
        p = jnp.exp(s - m_new)
        l = alpha * l + jnp.sum(p, axis=1, keepdims=True)
        acc = alpha * acc + jnp.dot(p.astype(BF16), v, preferred_element_type=F32)
        return m_new, l, acc

    init = (jnp.full((tq, 1), NEG_BIG, F32), jnp.zeros((tq, 1), F32), jnp.zeros((tq, V_HEAD_DIM), F32))
    m, l, acc = lax.fori_loop(0, qi + 1, body, init)
    o_ref[...] = (acc / l).astype(BF16)


def _flash(q, k, v, b, t, tq):
    nq = t // tq
    return pl.pallas_call(
        functools.partial(_flash_kernel, tq=tq),
        grid=(b, N_MLA_HEADS, nq),
        in_specs=[pl.BlockSpec((tq, QK_PAD), lambda i, h, j: (i * nq + j, h)),
                  pl.BlockSpec((t, QK_PAD), lambda i, h, j: (i, h)),
                  pl.BlockSpec((t, V_HEAD_DIM), lambda i, h, j: (i, h))],
        out_specs=pl.BlockSpec((tq, V_HEAD_DIM), lambda i, h, j: (i * nq + j, h)),
        out_shape=jax.ShapeDtypeStruct((b * t, N_MLA_HEADS * V_HEAD_DIM), BF16),
        compiler_params=_cparams(("parallel", "parallel", "arbitrary"), 48),
        name="flash_prompt",
    )(q, k, v)


def _qlat_kernel(qn_ref, qr_ref, w_ref, o_ref):
    o_ref[:, :KV_LORA] = jnp.dot(qn_ref[...], w_ref[0], preferred_element_type=F32).astype(BF16)
    o_ref[:, KV_LORA:] = qr_ref[...]


def _q_latent(q, w_uk_t):
    nb = q.shape[0]
    return pl.pallas_call(
        _qlat_kernel,
        grid=(N_MLA_HEADS,),
        in_specs=[pl.BlockSpec((nb, QK_NOPE_DIM), lambda h: (0, 2 * h)),
                  pl.BlockSpec((nb, LANE), lambda h: (0, 2 * h + 1)),
                  pl.BlockSpec((1, QK_NOPE_DIM, KV_LORA), lambda h: (h, 0, 0))],
        out_specs=pl.BlockSpec((nb, QLAT_W), lambda h: (0, h)),
        out_shape=jax.ShapeDtypeStruct((nb, N_MLA_HEADS * QLAT_W), BF16),
        compiler_params=_cparams(("parallel",), 32),
        name="q_latent",
    )(q, q, w_uk_t)


def _vup_kernel(o_ref, w_ref, out_ref):
    out_ref[...] = jnp.dot(o_ref[...], w_ref[0], preferred_element_type=F32).astype(BF16)


def _v_up(o_lat, w_uv_h):
    nb = o_lat.shape[0]
    return pl.pallas_call(
        _vup_kernel,
        grid=(N_MLA_HEADS,),
        in_specs=[pl.BlockSpec((nb, KV_LORA), lambda h: (0, h)),
                  pl.BlockSpec((1, KV_LORA, V_HEAD_DIM), lambda h: (h, 0, 0))],
        out_specs=pl.BlockSpec((nb, V_HEAD_DIM), lambda h: (0, h)),
        out_shape=jax.ShapeDtypeStruct((nb, N_MLA_HEADS * V_HEAD_DIM), BF16),
        compiler_params=_cparams(("parallel",), 32),
        name="v_up",
    )(o_lat, w_uv_h)


def _page_copies(pt_ref, ckv_hbm, kr_hbm, cbuf, rbuf, sems, first_page, slot, pages):
    copies = []
    for j in range(pages):
        pg = pt_ref[first_page + j]
        copies.append(pltpu.make_async_copy(ckv_hbm.at[0, pg], cbuf.at[slot, j], sems.at[0, slot]))
        copies.append(pltpu.make_async_copy(kr_hbm.at[0, pg], rbuf.at[slot, j], sems.at[1, slot]))
    return copies


def _decode_kernel(pt_ref, q_ref, cnew_ref, rnew_ref, ckv_hbm, kr_hbm, o_ref,
                   cbuf, rbuf, sems, m_scr, l_scr, acc_scr, *, pages):
    s = pl.program_id(1)
    ns = pl.num_programs(1)
    step = pl.program_id(0) * ns + s
    slot = lax.rem(step, 2)
    fetch = functools.partial(_page_copies, pt_ref, ckv_hbm, kr_hbm, cbuf, rbuf, sems, pages=pages)

    @pl.when(step == 0)
    def _():
        for cp in fetch(0, 0):
            cp.start()

    @pl.when(step + 1 < pl.num_programs(0) * ns)
    def _():
        for cp in fetch((step + 1) * pages, 1 - slot):
            cp.start()

    for cp in fetch(step * pages, slot):
        cp.wait()

    @pl.when(s == 0)
    def _():
        m_scr[...] = jnp.full_like(m_scr, NEG_BIG)
        l_scr[...] = jnp.zeros_like(l_scr)
        acc_scr[...] = jnp.zeros_like(acc_scr)

    q = q_ref[0]
    ql = q[:, :KV_LORA]
    qr = q[:, KV_LORA:KV_LORA + QK_ROPE_DIM]
    cs = [cbuf[slot, j].astype(BF16) for j in range(pages)]
    sc = [(_dot_nt(ql, cs[j]) + _dot_nt(qr, rbuf[slot, j].astype(BF16))) * ATTN_SCALE
          for j in range(pages)]
    sc = jnp.concatenate(sc, axis=1)
    m_old = m_scr[...]
    m_new = jnp.maximum(m_old, jnp.max(sc, axis=1, keepdims=True))
    alpha = jnp.exp(m_old - m_new)
    p = jnp.exp(sc - m_new)
    l_scr[...] = alpha * l_scr[...] + jnp.sum(p, axis=1, keepdims=True)
    pb = p.astype(BF16)
    pv = jnp.dot(pb[:, :PAGE_SIZE], cs[0], preferred_element_type=F32)
    for j in range(1, pages):
        pv = pv + jnp.dot(pb[:, j * PAGE_SIZE:(j + 1) * PAGE_SIZE], cs[j], preferred_element_type=F32)
    acc_scr[...] = alpha * acc_scr[...] + pv
    m_scr[...] = m_new

    @pl.when(s == pl.num_programs(1) - 1)
    def _():
        cn = cnew_ref[0].astype(BF16).astype(F32)
        rn = rnew_ref[0].astype(BF16).astype(F32)
        s_new = (jnp.sum(ql.astype(F32) * cn, axis=1, keepdims=True)
                 + jnp.sum(qr.astype(F32) * rn, axis=1, keepdims=True)) * ATTN_SCALE
        m_old = m_scr[...]
        m_fin = jnp.maximum(m_old, s_new)
        alpha = jnp.exp(m_old - m_fin)
        p_new = jnp.exp(s_new - m_fin)
        l_fin = alpha * l_scr[...] + p_new
        acc = alpha * acc_scr[...] + p_new.astype(BF16).astype(F32) * cn
        o_ref[0] = (acc / l_fin).astype(BF16)


def _decode(q_cat, ckv_new, kr_new, cache_ckv, cache_krope, page_table, pages):
    nb, n_pages = page_table.shape
    steps = n_pages // pages

    grid_spec = pltpu.PrefetchScalarGridSpec(
        num_scalar_prefetch=1,
        grid=(nb, steps),
        in_specs=[pl.BlockSpec((1, N_MLA_HEADS, QLAT_W), lambda b, s, pt: (b, 0, 0)),
                  pl.BlockSpec((1, 1, KV_LORA), lambda b, s, pt: (b, 0, 0)),
                  pl.BlockSpec((1, 1, QK_ROPE_DIM), lambda b, s, pt: (b, 0, 0)),
                  pl.BlockSpec(memory_space=pl.ANY), pl.BlockSpec(memory_space=pl.ANY)],
        out_specs=pl.BlockSpec((1, N_MLA_HEADS, KV_LORA), lambda b, s, pt: (b, 0, 0)),
        scratch_shapes=[pltpu.VMEM((2, pages, PAGE_SIZE, KV_LORA), F32),
                        pltpu.VMEM((2, pages, PAGE_SIZE, QK_ROPE_DIM), F32),
                        pltpu.SemaphoreType.DMA((2, 2)),
                        pltpu.VMEM((N_MLA_HEADS, 1), F32), pltpu.VMEM((N_MLA_HEADS, 1), F32),
                        pltpu.VMEM((N_MLA_HEADS, KV_LORA), F32)],
    )
    return pl.pallas_call(
        functools.partial(_decode_kernel, pages=pages),
        grid_spec=grid_spec,
        out_shape=jax.ShapeDtypeStruct((nb, N_MLA_HEADS, KV_LORA), BF16),
        compiler_params=_cparams(("arbitrary", "arbitrary"), 48),
        name="decode_attn",
    )(page_table.reshape(-1), q_cat, ckv_new, kr_new, cache_ckv, cache_krope)


def _layer_norm(x, g, b):
    mu = jnp.mean(x, axis=1, keepdims=True)
    xc = x - mu
    var = jnp.mean(xc * xc, axis=1, keepdims=True)
    return xc * lax.rsqrt(var + NORM_EPS) * g + b


def _outproj_kernel(*refs, tn, aliased):
    if aliased:
        refs = refs[1:]
    ssd_ref, mla_ref, wt_ref, wb_ref, x_ref, g_ref, b_ref, o_ref = refs
    j = pl.program_id(1)
    mix = (jnp.dot(ssd_ref[...], wt_ref[...], preferred_element_type=F32)
           + jnp.dot(mla_ref[...], wb_ref[...], preferred_element_type=F32))
    val = DEEPNORM_ALPHA * x_ref[...] + mix
    nj = D_MODEL // tn
    for jj in range(nj):
        @pl.when(j == jj)
        def _(jj=jj):
            o_ref[:, jj * tn:(jj + 1) * tn] = val

    @pl.when(j == nj - 1)
    def _():
        o_ref[...] = _layer_norm(o_ref[...], g_ref[...], b_ref[...])


def _outproj_ln(ssd, mla, w_out_b, x, g, b, tm, total_rows, row_block0, buf=None):
    m = x.shape[0]
    tn = 1024
    aliased = buf is not None
    in_specs = [pl.BlockSpec((tm, D_SSD), lambda i, j: (i, 0)),
                pl.BlockSpec((tm, D_SSD), lambda i, j: (i, 0)),
                pl.BlockSpec((D_SSD, tn), lambda i, j: (0, j)),
                pl.BlockSpec((D_SSD, tn), lambda i, j: (1, j)),
                pl.BlockSpec((tm, tn), lambda i, j: (i, j)),
                pl.BlockSpec((1, D_MODEL), lambda i, j: (0, 0)),
                pl.BlockSpec((1, D_MODEL), lambda i, j: (0, 0))]
    args = [ssd, mla, w_out_b, w_out_b, x, g, b]
    if aliased:
        in_specs = [pl.BlockSpec(memory_space=pl.ANY)] + in_specs
        args = [buf] + args
    return pl.pallas_call(
        functools.partial(_outproj_kernel, tn=tn, aliased=aliased),
        grid=(m // tm, D_MODEL // tn),
        in_specs=in_specs,
        out_specs=pl.BlockSpec((tm, D_MODEL), lambda i, j: (i + row_block0, 0)),
        out_shape=jax.ShapeDtypeStruct((total_rows, D_MODEL), F32),
        input_output_aliases={0: 0} if aliased else {},
        compiler_params=_cparams(("parallel", "arbitrary"), 56),
        name="out_proj_ln",
    )(*args)


def _router_kernel(h_ref, wh_ref, wm_ref, bias_ref, ids_ref, gates_ref):
    hh, hm, _ = _split3(h_ref[...])
    wh = wh_ref[...]
    d = functools.partial(jnp.dot, preferred_element_type=F32)
    lg = d(hh, wh) + (d(hh, wm_ref[...]) + d(hm, wh)) + bias_ref[...]
    lane = lax.broadcasted_iota(jnp.int32, lg.shape, 1)
    big = jnp.int32(1 << 20)
    gmask = lane < N_EXPERT_GROUPS
    gl = jnp.where(gmask, lg, -jnp.inf)
    ge = jnp.exp(gl - jnp.max(gl, axis=1, keepdims=True))
    gp = ge / jnp.sum(ge, axis=1, keepdims=True)
    g_val = jnp.max(gp, axis=1, keepdims=True)
    g_idx = jnp.min(jnp.where(gmask & (gp == g_val), lane, big), axis=1, keepdims=True)
    eid = lane - N_EXPERT_GROUPS
    emask = (eid >= 0) & (eid < N_EXPERTS) & (lax.shift_right_arithmetic(eid, 3) == g_idx)
    el = jnp.where(emask, lg, -jnp.inf)
    ee = jnp.exp(el - jnp.max(el, axis=1, keepdims=True))
    ep = jnp.where(emask, ee / jnp.sum(ee, axis=1, keepdims=True), -1.0)
    v1 = jnp.max(ep, axis=1, keepdims=True)
    i1 = jnp.min(jnp.where(emask & (ep == v1), lane, big), axis=1, keepdims=True)
    ep2 = jnp.where(lane == i1, -1.0, ep)
    v2 = jnp.max(ep2, axis=1, keepdims=True)
    i2 = jnp.min(jnp.where(emask & (lane != i1) & (ep2 == v2), lane, big), axis=1, keepdims=True)
    tot = v1 + v2
    ids_ref[...] = jnp.where(lane == 0, i1 - N_EXPERT_GROUPS,
                             jnp.where(lane == 1, i2 - N_EXPERT_GROUPS, 0))
    gates_ref[...] = jnp.where(lane == 0, g_val * (v1 / tot), jnp.where(lane == 1, g_val * (v2 / tot), 0.0))


def _router(h, wh, wm, bias, tm):
    t = h.shape[0]
    const = lambda shape: pl.BlockSpec(shape, lambda i: (0, 0))
    return pl.pallas_call(
        _router_kernel,
        grid=(t // tm,),
        in_specs=[pl.BlockSpec((tm, D_MODEL), lambda i: (i, 0)),
                  const((D_MODEL, LANE)), const((D_MODEL, LANE)), const((1, LANE))],
        out_specs=[pl.BlockSpec((tm, LANE), lambda i: (i, 0)), pl.BlockSpec((tm, LANE), lambda i: (i, 0))],
        out_shape=[jax.ShapeDtypeStruct((t, LANE), jnp.int32), jax.ShapeDtypeStruct((t, LANE), F32)],
        compiler_params=_cparams(("parallel",), 32),
        name="router",
    )(h, wh, wm, bias)


def _row_copy(src, src_row, dst, dst_row, sem):
    return pltpu.make_async_copy(src.at[pl.ds(src_row, 1)], dst.at[pl.ds(dst_row, 1)], sem)


def _moe_kernel(blk_e, blk_start, blk_n, n_live, tok, dst, h_hbm, wg_ref, wu_ref, wd_ref, y_hbm,
                xg, xb, acc, sem_in, sem_out):
    i = pl.program_id(0)
    n = pl.program_id(1)
    nrows = blk_n[i]
    start = blk_start[i]

    @pl.when((i == 0) & (n == 0))
    def _():
        xg[...] = jnp.zeros_like(xg)

    @pl.when((n == 0) & (nrows > 0))
    def _():
        def issue(r, carry):
            _row_copy(h_hbm, tok[start + r], xg, r, sem_in).start()
            return carry

        lax.fori_loop(0, nrows, issue, 0)

        def wait(r, carry):
            _row_copy(h_hbm, 0, xg, 0, sem_in).wait()
            return carry

        lax.fori_loop(0, nrows, wait, 0)
        xb[...] = xg[...].astype(BF16)

    @pl.when(nrows > 0)
    def _():
        wg = wg_ref[...].astype(BF16)
        wu = wu_ref[...].astype(BF16)
        wd = wd_ref[...].astype(BF16)
        for sub in range(MOE_BLK // MOE_SUB):
            rs = slice(sub * MOE_SUB, (sub + 1) * MOE_SUB)

            @pl.when(sub * MOE_SUB < nrows)
            def _(rs=rs):
                x = xb[rs, :]
                hid = _silu(jnp.dot(x, wg, preferred_element_type=F32)) * jnp.dot(
                    x, wu, preferred_element_type=F32)
                part = jnp.dot(hid.astype(BF16), wd, preferred_element_type=F32)

                @pl.when(n == 0)
                def _():
                    acc[rs, :] = part

                @pl.when(n > 0)
                def _():
                    acc[rs, :] = acc[rs, :] + part

    @pl.when((n == pl.num_programs(1) - 1) & (nrows > 0))
    def _():
        def issue(r, carry):
            _row_copy(acc, r, y_hbm, dst[start + r], sem_out).start()
            return carry

        lax.fori_loop(0, nrows, issue, 0)

        def wait(r, carry):
            _row_copy(acc, 0, y_hbm, 0, sem_out).wait()
            return carry

        lax.fori_loop(0, nrows, wait, 0)


def _moe(h_all, blk_e, blk_start, blk_n, n_live, tok, dst, w_gate_e, w_up_e, w_down_e, nb_max):
    t = h_all.shape[0]
    nt = D_EXPERT // MOE_TN

    def live_n(i, n, nl):
        return jnp.where(i < nl[0], n, nt - 1)

    grid_spec = pltpu.PrefetchScalarGridSpec(
        num_scalar_prefetch=6,
        grid=(nb_max, nt),
        in_specs=[pl.BlockSpec(memory_space=pl.ANY),
                  pl.BlockSpec((None, None, D_MODEL, MOE_TN),
                               lambda i, n, be, bs, bn, nl, tk, ds: (0, be[i], 0, live_n(i, n, nl))),
                  pl.BlockSpec((None, None, D_MODEL, MOE_TN),
                               lambda i, n, be, bs, bn, nl, tk, ds: (0, be[i], 0, live_n(i, n, nl))),
                  pl.BlockSpec((None, None, MOE_TN, D_MODEL),
                               lambda i, n, be, bs, bn, nl, tk, ds: (0, be[i], live_n(i, n, nl), 0))],
        out_specs=pl.BlockSpec(memory_space=pl.ANY),
        scratch_shapes=[pltpu.VMEM((MOE_BLK, D_MODEL), F32), pltpu.VMEM((MOE_BLK, D_MODEL), BF16),
                        pltpu.VMEM((MOE_BLK, D_MODEL), F32),
                        pltpu.SemaphoreType.DMA(()), pltpu.SemaphoreType.DMA(())],
    )
    return pl.pallas_call(
        _moe_kernel,
        grid_spec=grid_spec,
        out_shape=jax.ShapeDtypeStruct((TOP_K * t, D_MODEL), F32),
        compiler_params=_cparams(("arbitrary", "arbitrary"), 56),
        name="moe_experts",
    )(blk_e, blk_start, blk_n, n_live, tok, dst, h_all, w_gate_e, w_up_e, w_down_e)


def _dispatch_plan(ids, nb_max):
    t = ids.shape[0]
    n_assign = TOP_K * t
    e_flat = ids.reshape(-1)
    order = jnp.argsort(e_flat).astype(jnp.int32)
    counts = jnp.bincount(e_flat, length=N_EXPERTS).astype(jnp.int32)
    start = jnp.cumsum(counts) - counts
    nblk = (counts + MOE_BLK - 1) // MOE_BLK
    blk_end = jnp.cumsum(nblk)
    n_live = blk_end[-1]
    bi = jnp.arange(nb_max, dtype=jnp.int32)
    live = bi < n_live
    be = jnp.minimum(jnp.searchsorted(blk_end, bi, side='right'), N_EXPERTS - 1).astype(jnp.int32)
    last_e = jnp.max(jnp.where(counts > 0, jnp.arange(N_EXPERTS, dtype=jnp.int32), 0))
    be = jnp.where(live, be, last_e)
    k = bi - (blk_end[be] - nblk[be])
    bstart = jnp.where(live, start[be] + k * MOE_BLK, 0).astype(jnp.int32)
    bn = jnp.where(live, jnp.clip(counts[be] - k * MOE_BLK, 0, MOE_BLK), 0).astype(jnp.int32)
    tok = order // TOP_K
    dst = (order % TOP_K) * t + tok
    return be, bstart, bn, n_live.reshape(1).astype(jnp.int32), tok, dst


def _ln2_kernel(h_ref, y0_ref, y1_ref, gt_ref, g_ref, b_ref, op_ref, os_ref, *, n_prompt_blocks):
    i = pl.program_id(0)
    gt = gt_ref[...]
    ffn = gt[:, 0:1] * y0_ref[...] + gt[:, 1:2] * y1_ref[...]
    out = _layer_norm(DEEPNORM_ALPHA * h_ref[...] + ffn, g_ref[...], b_ref[...])

    @pl.when(i < n_prompt_blocks)
    def _():
        op_ref[...] = out

    @pl.when(i >= n_prompt_blocks)
    def _():
        os_ref[...] = out


def _ln2(h_all, y2, gates, g, b, n_prompt, tm):
    t = h_all.shape[0]
    nblk = t // tm
    npb = n_prompt // tm
    const = lambda shape: pl.BlockSpec(shape, lambda i: (0, 0))
    return pl.pallas_call(
        functools.partial(_ln2_kernel, n_prompt_blocks=npb),
        grid=(nblk,),
        in_specs=[pl.BlockSpec((tm, D_MODEL), lambda i: (i, 0)),
                  pl.BlockSpec((tm, D_MODEL), lambda i: (i, 0)),
                  pl.BlockSpec((tm, D_MODEL), lambda i: (i + nblk, 0)),
                  pl.BlockSpec((tm, LANE), lambda i: (i, 0)),
                  const((1, D_MODEL)), const((1, D_MODEL))],
        out_specs=[pl.BlockSpec((tm, D_MODEL), lambda i: (jnp.minimum(i, npb - 1), 0)),
                   pl.BlockSpec((tm, D_MODEL), lambda i: (jnp.maximum(i - npb, 0), 0))],
        out_shape=[jax.ShapeDtypeStruct((n_prompt, D_MODEL), F32),
                   jax.ShapeDtypeStruct((t - n_prompt, D_MODEL), F32)],
        compiler_params=_cparams(("arbitrary",), 48),
        name="ffn_ln",
    )(h_all, y2, y2, gates, g, b)


def _rot_cols(w):
    half = w.shape[-1] // 2
    return jnp.concatenate([-w[..., half:], w[..., :half]], axis=-1)


def _rope_table(pos):
    half = QK_ROPE_DIM // 2
    inv_freq = ROPE_THETA ** (-jnp.arange(half, dtype=F32) / half)
    ang = pos.astype(F32)[:, None] * inv_freq
    cos, sin = jnp.cos(ang), jnp.sin(ang)
    return jnp.concatenate([cos, cos, sin, sin], axis=1)


def _expander(width):
    k = lax.broadcasted_iota(jnp.int32, (LANE, N_SSD_HEADS * width), 0)
    c = lax.broadcasted_iota(jnp.int32, (LANE, N_SSD_HEADS * width), 1)
    return (k == c // width).astype(BF16)


def _row_tile(m, pref):
    while m % pref:
        pref //= 2
    return pref


def kernel(x_prompt, x_sample, cache_ckv, cache_krope, state_ssm, state_conv, page_table, w_in, conv_w, conv_b, dt_bias, a_log, d_skip, ssd_norm_w, q_norm_w, w_uq, kv_norm_w, w_uk, w_uv, w_out, ln1_g, ln1_b, w_router_group, b_router_group, w_router_expert, b_router_expert, w_gate_e, w_up_e, w_down_e, ln2_g, ln2_b):
    b, t, _ = x_prompt.shape
    nb, dec_t, _ = x_sample.shape
    assert dec_t == 1 and w_in.shape[0] == 1
    n_pages = page_table.shape[1]
    past = n_pages * PAGE_SIZE
    n_prompt = b * t
    n_tok = n_prompt + nb

    w = w_in[0]
    s0 = D_SSD
    s1 = s0 + CONV_CH
    s2 = s1 + N_SSD_HEADS
    s3 = s2 + Q_LORA
    s4 = s3 + KV_LORA
    w_z, w_xbc, w_dt, w_cq, w_ckv, w_kr = (w[:, :s0], w[:, s0:s1], w[:, s1:s2], w[:, s2:s3],
                                           w[:, s3:s4], w[:, s4:])
    w_all = jnp.concatenate(
        [w_xbc, w_cq, w_z, w_ckv, w_kr, _rot_cols(w_kr), w_dt,
         jnp.zeros((D_MODEL, LANE - N_SSD_HEADS), F32)], axis=1).astype(BF16)
    pad_heads = lambda v: jnp.pad(v, (0, LANE - N_SSD_HEADS)).reshape(1, LANE)
    dtb = pad_heads(dt_bias[0])
    a_neg = pad_heads(-jnp.exp(a_log[0].astype(F32)))
    dsk = jnp.repeat(d_skip[0], SSD_HEAD_DIM).reshape(1, D_SSD)
    nw = ssd_norm_w[0].reshape(1, D_SSD)
    cw = conv_w[0]
    cb = conv_b[0].reshape(1, CONV_CH)
    e64 = _expander(SSD_HEAD_DIM)
    e128 = _expander(LANE)
    wq = w_uq[0]
    wq_r = wq[..., QK_NOPE_DIM:]
    wq_all = jnp.concatenate([wq[..., :QK_NOPE_DIM], wq_r, _rot_cols(wq_r)], axis=-1)
    wq_all = wq_all.reshape(Q_LORA, N_MLA_HEADS * QK_PAD).astype(BF16)
    wkv = jnp.concatenate([w_uk[0].reshape(KV_LORA, -1), w_uv[0].reshape(KV_LORA, -1)], axis=1).astype(BF16)
    w_uk_t = jnp.transpose(w_uk[0], (1, 2, 0)).astype(BF16)
    w_uv_h = jnp.transpose(w_uv[0], (1, 0, 2)).astype(BF16)
    w_out_b = w_out[0].astype(BF16)
    qnw = q_norm_w[0].reshape(1, Q_LORA)
    kvnw = kv_norm_w[0].reshape(1, KV_LORA)
    g1, b1 = ln1_g[0].reshape(1, D_MODEL), ln1_b[0].reshape(1, D_MODEL)
    g2, b2 = ln2_g[0].reshape(1, D_MODEL), ln2_b[0].reshape(1, D_MODEL)
    w_r = jnp.concatenate([w_router_group[0], w_router_expert[0],
                           jnp.zeros((D_MODEL, LANE - N_EXPERT_GROUPS - N_EXPERTS), F32)], axis=1)
    w_r_hi = w_r.astype(BF16)
    w_r_mid = (w_r - w_r_hi.astype(F32)).astype(BF16)
    b_r = jnp.concatenate([b_router_group[0], b_router_expert[0],
                           jnp.zeros((LANE - N_EXPERT_GROUPS - N_EXPERTS,), F32)]).reshape(1, LANE)

    xp = x_prompt.reshape(n_prompt, D_MODEL)
    xsm = x_sample.reshape(nb, D_MODEL)

    proj_p = _matmul(xp, w_all, _row_tile(n_prompt, 512), 768)
    ssd_p, ssm_p, conv_p = _ssd_prompt(proj_p, b, t, cw, cb, dtb, a_neg, dsk, nw, e64, e128)
    tm_p = _row_tile(t, 256)
    tab_p = _rope_table(jnp.arange(t))
    q_p, ckv_p, kr_p, k_p, v_p = _mla_prep(proj_p, tab_p, qnw, kvnw, wq_all, wkv, tm_p, True)
    mla_p = _flash(q_p, k_p, v_p, b, t, _row_tile(t, 512))
    tm_o = _row_tile(n_prompt, 256)
    h_all = _outproj_ln(ssd_p, mla_p, w_out_b, xp, g1, b1, tm_o, n_tok, 0,
                        buf=jnp.zeros((n_tok, D_MODEL), F32))

    proj_s = _matmul(xsm, w_all, nb, 768)
    conv_buf_t = jnp.transpose(state_conv[0], (1, 0, 2))
    ssd_s, ssm_s, conv_s_t = _ssd_sample(proj_s, conv_buf_t, state_ssm[0], cw, cb, dtb, a_neg, dsk, nw, e64)
    tab_s = _rope_table(jnp.full((nb,), past, jnp.int32))
    q_s, ckv_s, kr_s = _mla_prep(proj_s, tab_s, qnw, kvnw, wq_all, wkv, nb, False)
    q_cat = _q_latent(q_s, w_uk_t).reshape(nb, N_MLA_HEADS, QLAT_W)
    pages = math.gcd(n_pages, 16)
    o_lat = _decode(q_cat, ckv_s.reshape(nb, 1, KV_LORA), kr_s.reshape(nb, 1, QK_ROPE_DIM),
                    cache_ckv, cache_krope, page_table, pages)
    mla_s = _v_up(o_lat.reshape(nb, N_MLA_HEADS * KV_LORA), w_uv_h)
    h_all = _outproj_ln(ssd_s, mla_s, w_out_b, xsm, g1, b1, nb, n_tok, n_prompt // nb, buf=h_all)

    tm_r = _row_tile(math.gcd(n_prompt, nb), 128)
    ids, gates = _router(h_all, w_r_hi, w_r_mid, b_r, tm_r)
    nb_max = (TOP_K * n_tok) // MOE_BLK + N_EXPERTS
    be, bstart, bn, n_live, tok, dst = _dispatch_plan(ids[:, :TOP_K], nb_max)
    y2 = _moe(h_all, be, bstart, bn, n_live, tok, dst, w_gate_e, w_up_e, w_down_e, nb_max)
    y_p, y_s = _ln2(h_all, y2, gates, g2, b2, n_prompt, tm_r)

    return (y_p.reshape(b, t, D_MODEL), y_s.reshape(nb, 1, D_MODEL),
            ckv_p.reshape(1, b, t, KV_LORA), kr_p.reshape(1, b, t, QK_ROPE_DIM),
            ssm_p[None], conv_p[None],
            ckv_s.reshape(1, nb, 1, KV_LORA), kr_s.reshape(1, nb, 1, QK_ROPE_DIM),
            ssm_s[None], jnp.transpose(conv_s_t, (1, 0, 2))[None])
```

```python
import functools
import math

import jax
import jax.numpy as jnp
from jax import lax
from jax.experimental import pallas as pl
from jax.experimental.pallas import tpu as pltpu

F32 = jnp.float32
BF16 = jnp.bfloat16

D_MODEL = 4096
D_SSD = 2048
SSD_HEAD_DIM = 64
N_SSD_HEADS = 32
N_SSD_GROUPS = 4
GROUP_W = D_SSD // N_SSD_GROUPS
D_STATE = 128
CONV_W = 4
CONV_CH = D_SSD + 2 * N_SSD_GROUPS * D_STATE
SSD_CHUNK = 128
N_MLA_HEADS = 16
V_HEAD_DIM = 128
QK_NOPE_DIM = 128
QK_ROPE_DIM = 64
Q_LORA = 1024
KV_LORA = 512
ROPE_THETA = 10000.0
ATTN_SCALE = (QK_NOPE_DIM + QK_ROPE_DIM) ** -0.5
PAGE_SIZE = 128
N_EXPERT_GROUPS = 8
EXPERTS_PER_GROUP = 8
N_EXPERTS = 64
TOP_K = 2
D_EXPERT = 1024
DEPTH = 1
DEEPNORM_ALPHA = (2 * DEPTH) ** 0.25
NORM_EPS = 1e-5

LANE = 128
SUBLANE = 8
QK_PAD = 256
QLAT_W = 640

C_XBC, C_CQ, C_Z, C_CKV, C_KR, C_DT = 0, 3072, 4096, 6144, 6656, 6784
PROJ_W = 6912

DECODE_PAGES = 32
MOE_BLK = 512
MOE_SUB = 64
MOE_TN = 256
MOE_DOWN_TN = 1024
ROW_UNROLL = 8
NEG_BIG = -1e30
LOG2_E = 1.4426950408889634


def _cparams(sem, vmem_mb):
    return pltpu.CompilerParams(dimension_semantics=sem, vmem_limit_bytes=vmem_mb * 1024 * 1024)


def _sigmoid(x):
    return 1.0 / (1.0 + jnp.exp(-x))


def _silu(x):
    return x * _sigmoid(x)


def _split3(x):
    hi = x.astype(BF16)
    r1 = x - hi.astype(F32)
    mid = r1.astype(BF16)
    lo = (r1 - mid.astype(F32)).astype(BF16)
    return hi, mid, lo


def _expand(x, e):
    hi, mid, lo = _split3(x)
    d = functools.partial(jnp.dot, preferred_element_type=F32)
    return d(hi, e) + d(mid, e) + d(lo, e)


def _dot_nt(a, b):
    return lax.dot_general(a, b, (((1,), (1,)), ((), ())), preferred_element_type=F32)


def _dot_tn(a, b):
    return lax.dot_general(a, b, (((0,), (0,)), ((), ())), preferred_element_type=F32)


def _mm_kernel(x_ref, w_ref, o_ref, xb_ref):
    @pl.when(pl.program_id(1) == 0)
    def _():
        xb_ref[...] = x_ref[...].astype(BF16)

    o_ref[...] = _dot_nt(xb_ref[...], w_ref[...])


def _matmul(x, w_t, tm, tn):
    m, k = x.shape
    n = w_t.shape[0]
    return pl.pallas_call(
        _mm_kernel,
        grid=(m // tm, n // tn),
        in_specs=[pl.BlockSpec((tm, k), lambda i, j: (i, 0)),
                  pl.BlockSpec((tn, k), lambda i, j: (j, 0))],
        out_specs=pl.BlockSpec((tm, tn), lambda i, j: (i, j)),
        out_shape=jax.ShapeDtypeStruct((m, n), F32),
        scratch_shapes=[pltpu.VMEM((tm, k), BF16)],
        compiler_params=_cparams(("parallel", "arbitrary"), 48),
        name="in_proj",
    )(x, w_t)


def _softplus(x):
    return jnp.maximum(x, 0.0) + jnp.log1p(jnp.exp(-jnp.abs(x)))


def _gated_norm(y, xs, z, dsk, nw):
    y = (y + dsk * xs) * _silu(z)
    outs = []
    for g in range(N_SSD_GROUPS):
        yg = y[:, g * GROUP_W:(g + 1) * GROUP_W]
        ms = jnp.mean(yg * yg, axis=1, keepdims=True)
        outs.append(yg * lax.rsqrt(ms + NORM_EPS))
    return jnp.concatenate(outs, axis=1) * nw


def _cumsum_rows(x):
    row = lax.broadcasted_iota(jnp.int32, x.shape, 0)
    s = 1
    while s < x.shape[0]:
        x = x + jnp.where(row >= s, pltpu.roll(x, s, 0), 0.0)
        s *= 2
    return x


def _ssd_prompt_kernel(xbc_ref, z_ref, dt_ref, cw_ref, cb_ref, dtb_ref, a_ref, dsk_ref, nw_ref,
                       e64_ref, e128_ref, out_ref, st_ref, conv_ref, xs_scr, st_scr):
    c = pl.program_id(1)
    q = SSD_CHUNK

    @pl.when(c == 0)
    def _():
        xs_scr[0:SUBLANE, :] = jnp.zeros((SUBLANE, CONV_CH), F32)
        st_scr[...] = jnp.zeros_like(st_scr)

    xs_scr[SUBLANE:SUBLANE + q, :] = xbc_ref[...]
    acc = cb_ref[...] + cw_ref[CONV_W - 1:CONV_W, :] * xs_scr[SUBLANE:SUBLANE + q, :]
    for k in range(CONV_W - 1):
        off = SUBLANE - (CONV_W - 1) + k
        acc = acc + cw_ref[k:k + 1, :] * xs_scr[off:off + q, :]
    xbc = _silu(acc)
    conv_ref[0] = xs_scr[q + SUBLANE - (CONV_W - 1):q + SUBLANE, :]
    xs_scr[0:SUBLANE, :] = xs_scr[q:q + SUBLANE, :]

    xs = xbc[:, :D_SSD]
    bm = xbc[:, D_SSD:D_SSD + N_SSD_GROUPS * D_STATE]
    cm = xbc[:, D_SSD + N_SSD_GROUPS * D_STATE:]

    lane = lax.broadcasted_iota(jnp.int32, (q, LANE), 1)
    row = lax.broadcasted_iota(jnp.int32, (q, LANE), 0)
    dt = jnp.where(lane < N_SSD_HEADS, _softplus(dt_ref[...] + dtb_ref[...]), 0.0)
    a_cs = _cumsum_rows(dt * a_ref[...])
    a_cs_t = a_cs.T
    e64 = e64_ref[...]
    dt_e = _expand(dt, e64)
    acs_e = _expand(a_cs, e64)
    acs_b = _expand(a_cs, e128_ref[...])
    xdt = xs * dt_e
    last = acs_e[q - 1:q, :]
    xw = (xdt * jnp.exp(last - acs_e)).astype(BF16)
    xdt_b = xdt.astype(BF16)
    chunk_decay = jnp.exp(last)
    in_decay = jnp.exp(acs_e)
    causal = row >= lane

    ys = []
    for g in range(N_SSD_GROUPS):
        gs = slice(g * GROUP_W, (g + 1) * GROUP_W)
        bg = bm[:, g * D_STATE:(g + 1) * D_STATE].astype(BF16)
        cg = cm[:, g * D_STATE:(g + 1) * D_STATE].astype(BF16)
        cb = _dot_nt(cg, bg)
        st_g = st_scr[:, gs]
        y_off = jnp.dot(cg, st_g.astype(BF16), preferred_element_type=F32) * in_decay[:, gs]
        yd = []
        for pr in range(GROUP_W // LANE):
            h0 = g * (N_SSD_HEADS // N_SSD_GROUPS) + 2 * pr
            ms = []
            for h in (h0, h0 + 1):
                seg = acs_b[:, h * LANE:(h + 1) * LANE] - a_cs_t[h:h + 1, :]
                dec = jnp.exp(jnp.where(causal, seg, -jnp.inf))
                ms.append((cb * dec).astype(BF16))
            lhs = jnp.concatenate(ms, axis=1)
            xp = xdt_b[:, h0 * SSD_HEAD_DIM:(h0 + 2) * SSD_HEAD_DIM]
            zero = jnp.zeros_like(xp)
            rhs = jnp.concatenate([jnp.where(lane < SSD_HEAD_DIM, xp, zero),
                                   jnp.where(lane >= SSD_HEAD_DIM, xp, zero)], axis=0)
            yd.append(jnp.dot(lhs, rhs, preferred_element_type=F32))
        ys.append(jnp.concatenate(yd, axis=1) + y_off)
        st_scr[:, gs] = st_g * chunk_decay[:, gs] + _dot_tn(bg, xw[:, gs])
    y = jnp.concatenate(ys, axis=1)

    out_ref[...] = _gated_norm(y, xs, z_ref[...], dsk_ref[...], nw_ref[...]).astype(BF16)

    @pl.when(c == pl.num_programs(1) - 1)
    def _():
        st_ref[0] = st_scr[...].T.reshape(N_SSD_HEADS, SSD_HEAD_DIM, D_STATE)


def _ssd_prompt(proj, b, t, cw, cb, dtb, a, dsk, nw, e64, e128):
    nc = t // SSD_CHUNK
    q = SSD_CHUNK
    const = lambda shape: pl.BlockSpec(shape, lambda i, j: (0, 0))
    return pl.pallas_call(
        _ssd_prompt_kernel,
        grid=(b, nc),
        in_specs=[pl.BlockSpec((q, CONV_CH), lambda i, j: (i * nc + j, C_XBC // CONV_CH)),
                  pl.BlockSpec((q, D_SSD), lambda i, j: (i * nc + j, C_Z // D_SSD)),
                  pl.BlockSpec((q, LANE), lambda i, j: (i * nc + j, C_DT // LANE)),
                  const((CONV_W, CONV_CH)), const((1, CONV_CH)), const((1, LANE)), const((1, LANE)),
                  const((1, D_SSD)), const((1, D_SSD)), const((LANE, D_SSD)),
                  const((LANE, N_SSD_HEADS * LANE))],
        out_specs=[pl.BlockSpec((q, D_SSD), lambda i, j: (i * nc + j, 0)),
                   pl.BlockSpec((1, N_SSD_HEADS, SSD_HEAD_DIM, D_STATE), lambda i, j: (i, 0, 0, 0)),
                   pl.BlockSpec((1, CONV_W - 1, CONV_CH), lambda i, j: (i, 0, 0))],
        out_shape=[jax.ShapeDtypeStruct((b * t, D_SSD), BF16),
                   jax.ShapeDtypeStruct((b, N_SSD_HEADS, SSD_HEAD_DIM, D_STATE), F32),
                   jax.ShapeDtypeStruct((b, CONV_W - 1, CONV_CH), F32)],
        scratch_shapes=[pltpu.VMEM((q + SUBLANE, CONV_CH), F32), pltpu.VMEM((D_STATE, D_SSD), F32)],
        compiler_params=_cparams(("parallel", "arbitrary"), 48),
        name="ssd_prompt",
    )(proj, proj, proj, cw, cb, dtb, a, dsk, nw, e64, e128)


def _ssd_sample_kernel(xbc_ref, z_ref, dt_ref, cbuf_ref, st_in_ref, cw_ref, cb_ref, dtb_ref, a_ref,
                       dsk_ref, nw_ref, e64_ref, out_ref, st_out_ref, conv_ref,
                       xdt_t, dec_t, b_scr, c_scr, xs_scr, y_scr):
    s = pl.program_id(0)
    n = pl.num_programs(0)
    rows = N_SSD_HEADS * SSD_HEAD_DIM

    @pl.when(s == 0)
    def _():
        x_new = xbc_ref[...]
        acc = cb_ref[...] + cw_ref[CONV_W - 1:CONV_W, :] * x_new
        for k in range(CONV_W - 1):
            acc = acc + cw_ref[k:k + 1, :] * cbuf_ref[k]
        xbc = _silu(acc)
        for k in range(CONV_W - 2):
            conv_ref[k] = cbuf_ref[k + 1]
        conv_ref[CONV_W - 2] = x_new
        xs = xbc[:, :D_SSD]
        lane = lax.broadcasted_iota(jnp.int32, dt_ref.shape, 1)
        dt = jnp.where(lane < N_SSD_HEADS, _softplus(dt_ref[...] + dtb_ref[...]), 0.0)
        e64 = e64_ref[...]
        dt_e = _expand(dt, e64)
        da_e = _expand(dt * a_ref[...], e64)
        xs_scr[...] = xs
        xdt_t[...] = (xs * dt_e).T
        dec_t[...] = jnp.exp(da_e).T
        b_scr[...] = xbc[:, D_SSD:D_SSD + N_SSD_GROUPS * D_STATE]
        c_scr[...] = xbc[:, D_SSD + N_SSD_GROUPS * D_STATE:]

    st = st_in_ref[0].reshape(rows, D_STATE)
    lane = lax.broadcasted_iota(jnp.int32, (rows, xdt_t.shape[1]), 1)
    pick = lane == s
    xcol = jnp.sum(jnp.where(pick, xdt_t[...], 0.0), axis=1, keepdims=True)
    dcol = jnp.sum(jnp.where(pick, dec_t[...], 0.0), axis=1, keepdims=True)
    brow = b_scr[pl.ds(s, 1), :]
    crow = c_scr[pl.ds(s, 1), :]
    bfull = jnp.concatenate(
        [jnp.broadcast_to(brow[:, g * D_STATE:(g + 1) * D_STATE], (GROUP_W, D_STATE))
         for g in range(N_SSD_GROUPS)], axis=0)
    st_new = st * dcol + xcol * bfull
    st_out_ref[0] = st_new.reshape(N_SSD_HEADS, SSD_HEAD_DIM, D_STATE)
    st_b = st_new.astype(BF16)
    ys = []
    for g in range(N_SSD_GROUPS):
        cg = jnp.broadcast_to(crow[:, g * D_STATE:(g + 1) * D_STATE], (2 * SUBLANE, D_STATE)).astype(BF16)
        yg = _dot_nt(cg, st_b[g * GROUP_W:(g + 1) * GROUP_W, :])
        ys.append(yg[0:1, :])
    y_scr[pl.ds(s, 1), :] = jnp.concatenate(ys, axis=1)

    @pl.when(s == n - 1)
    def _():
        out_ref[...] = _gated_norm(y_scr[...], xs_scr[...], z_ref[...], dsk_ref[...],
                                   nw_ref[...]).astype(BF16)


def _ssd_sample(proj, conv_buf_t, state, cw, cb, dtb, a, dsk, nw, e64):
    nb = proj.shape[0]
    const2 = lambda shape: pl.BlockSpec(shape, lambda i: (0, 0))
    return pl.pallas_call(
        _ssd_sample_kernel,
        grid=(nb,),
        in_specs=[pl.BlockSpec((nb, CONV_CH), lambda i: (0, C_XBC // CONV_CH)),
                  pl.BlockSpec((nb, D_SSD), lambda i: (0, C_Z // D_SSD)),
                  pl.BlockSpec((nb, LANE), lambda i: (0, C_DT // LANE)),
                  pl.BlockSpec((CONV_W - 1, nb, CONV_CH), lambda i: (0, 0, 0)),
                  pl.BlockSpec((1, N_SSD_HEADS, SSD_HEAD_DIM, D_STATE), lambda i: (i, 0, 0, 0)),
                  const2((CONV_W, CONV_CH)), const2((1, CONV_CH)), const2((1, LANE)), const2((1, LANE)),
                  const2((1, D_SSD)), const2((1, D_SSD)), const2((LANE, D_SSD))],
        out_specs=[pl.BlockSpec((nb, D_SSD), lambda i: (0, 0)),
                   pl.BlockSpec((1, N_SSD_HEADS, SSD_HEAD_DIM, D_STATE), lambda i: (i, 0, 0, 0)),
                   pl.BlockSpec((CONV_W - 1, nb, CONV_CH), lambda i: (0, 0, 0))],
        out_shape=[jax.ShapeDtypeStruct((nb, D_SSD), BF16),
                   jax.ShapeDtypeStruct(state.shape, F32),
                   jax.ShapeDtypeStruct((CONV_W - 1, nb, CONV_CH), F32)],
        scratch_shapes=[pltpu.VMEM((D_SSD, nb), F32), pltpu.VMEM((D_SSD, nb), F32),
                        pltpu.VMEM((nb, N_SSD_GROUPS * D_STATE), F32),
                        pltpu.VMEM((nb, N_SSD_GROUPS * D_STATE), F32),
                        pltpu.VMEM((nb, D_SSD), F32), pltpu.VMEM((nb, D_SSD), F32)],
        compiler_params=_cparams(("arbitrary",), 48),
        name="ssd_sample",
    )(proj, proj, proj, conv_buf_t, state, cw, cb, dtb, a, dsk, nw, e64)


def _rms(x, w):
    return x * lax.rsqrt(jnp.mean(x * x, axis=1, keepdims=True) + NORM_EPS) * w


def _rope_fold(t):
    lane = lax.broadcasted_iota(jnp.int32, t.shape, 1)
    return jnp.where(lane < QK_ROPE_DIM, t + pltpu.roll(t, QK_ROPE_DIM, 1), 0.0)


def _mla_prep_kernel(cq_ref, ckv_ref, kr_ref, tab_ref, qnw_ref, kvnw_ref, wq_ref, wkv_ref,
                     q_ref, ckvn_ref, krope_ref, *kv_refs, with_kv):
    tab = tab_ref[...]
    qn = _rms(cq_ref[...], qnw_ref[...]).astype(BF16)
    qf = jnp.dot(qn, wq_ref[...], preferred_element_type=F32)
    for h in range(N_MLA_HEADS):
        base = h * QK_PAD
        q_ref[:, base:base + QK_NOPE_DIM] = qf[:, base:base + QK_NOPE_DIM].astype(BF16)
        q_ref[:, base + QK_NOPE_DIM:base + QK_PAD] = _rope_fold(
            qf[:, base + QK_NOPE_DIM:base + QK_PAD] * tab).astype(BF16)
    ckvn = _rms(ckv_ref[...], kvnw_ref[...])
    ckvn_ref[...] = ckvn
    kro = _rope_fold(kr_ref[...] * tab)
    krope_ref[...] = kro[:, :QK_ROPE_DIM]
    if with_kv:
        k_ref, v_ref = kv_refs
        kv = jnp.dot(ckvn.astype(BF16), wkv_ref[...], preferred_element_type=F32)
        kro_b = kro.astype(BF16)
        for h in range(N_MLA_HEADS):
            k_ref[:, h * QK_PAD:h * QK_PAD + QK_NOPE_DIM] = kv[:, h * LANE:(h + 1) * LANE].astype(BF16)
            k_ref[:, h * QK_PAD + QK_NOPE_DIM:(h + 1) * QK_PAD] = kro_b
        v_ref[...] = kv[:, N_MLA_HEADS * LANE:].astype(BF16)


def _mla_prep(proj, tab, qnw, kvnw, wq, wkv, tm, with_kv):
    m = proj.shape[0]
    npos = tab.shape[0] // tm
    const = lambda shape: pl.BlockSpec(shape, lambda i: (0, 0))
    out_specs = [pl.BlockSpec((tm, N_MLA_HEADS * QK_PAD), lambda i: (i, 0)),
                 pl.BlockSpec((tm, KV_LORA), lambda i: (i, 0)),
                 pl.BlockSpec((tm, QK_ROPE_DIM), lambda i: (i, 0))]
    out_shape = [jax.ShapeDtypeStruct((m, N_MLA_HEADS * QK_PAD), BF16),
                 jax.ShapeDtypeStruct((m, KV_LORA), F32),
                 jax.ShapeDtypeStruct((m, QK_ROPE_DIM), F32)]
    if with_kv:
        out_specs += [pl.BlockSpec((tm, N_MLA_HEADS * QK_PAD), lambda i: (i, 0)),
                      pl.BlockSpec((tm, N_MLA_HEADS * V_HEAD_DIM), lambda i: (i, 0))]
        out_shape += [jax.ShapeDtypeStruct((m, N_MLA_HEADS * QK_PAD), BF16),
                      jax.ShapeDtypeStruct((m, N_MLA_HEADS * V_HEAD_DIM), BF16)]
    return pl.pallas_call(
        functools.partial(_mla_prep_kernel, with_kv=with_kv),
        grid=(m // tm,),
        in_specs=[pl.BlockSpec((tm, Q_LORA), lambda i: (i, C_CQ // Q_LORA)),
                  pl.BlockSpec((tm, KV_LORA), lambda i: (i, C_CKV // KV_LORA)),
                  pl.BlockSpec((tm, LANE), lambda i: (i, C_KR // LANE)),
                  pl.BlockSpec((tm, LANE), lambda i: (i % npos, 0)),
                  const((1, Q_LORA)), const((1, KV_LORA)),
                  const((Q_LORA, N_MLA_HEADS * QK_PAD)),
                  const((KV_LORA, 2 * N_MLA_HEADS * V_HEAD_DIM))],
        out_specs=out_specs,
        out_shape=out_shape,
        compiler_params=_cparams(("parallel",), 56),
        name="mla_prep",
    )(proj, proj, proj, tab, qnw, kvnw, wq, wkv)


def _flash_kernel(q_ref, k_ref, v_ref, o_ref, *, tq):
    qi = pl.program_id(2)
    q = q_ref[...]

    def block(j, carry, diagonal):
        m, l, acc = carry
        off = pl.multiple_of(j * tq, tq)
        k = k_ref[pl.ds(off, tq), :]
        v = v_ref[pl.ds(off, tq), :]
        s = _dot_nt(q, k) * (ATTN_SCALE * LOG2_E)
        if diagonal:
            row = lax.broadcasted_iota(jnp.int32, (tq, tq), 0)
            col = lax.broadcasted_iota(jnp.int32, (tq, tq), 1)
            s = jnp.where(col <= row, s, NEG_BIG)
        m_new = jnp.maximum(m, jnp.max(s, axis=1, keepdims=True))
        alpha = jnp.exp2(m - m_new)
        p = jnp.exp2(s - m_new)
        l = alpha * l + jnp.sum(p, axis=1, keepdims=True)
        acc = alpha * acc + jnp.dot(p.astype(BF16), v, preferred_element_type=F32)
        return m_new, l, acc

    init = (jnp.full((tq, 1), NEG_BIG, F32), jnp.zeros((tq, 1), F32), jnp.zeros((tq, V_HEAD_DIM), F32))
    carry = lax.fori_loop(0, qi, functools.partial(block, diagonal=False), init)
    m, l, acc = block(qi, carry, True)
    o_ref[...] = (acc / l).astype(BF16)


def _flash(q, k, v, b, t, tq):
    nq = t // tq
    return pl.pallas_call(
        functools.partial(_flash_kernel, tq=tq),
        grid=(b, N_MLA_HEADS, nq),
        in_specs=[pl.BlockSpec((tq, QK_PAD), lambda i, h, j: (i * nq + j, h)),
                  pl.BlockSpec((t, QK_PAD), lambda i, h, j: (i, h)),
                  pl.BlockSpec((t, V_HEAD_DIM), lambda i, h, j: (i, h))],
        out_specs=pl.BlockSpec((tq, V_HEAD_DIM), lambda i, h, j: (i * nq + j, h)),
        out_shape=jax.ShapeDtypeStruct((b * t, N_MLA_HEADS * V_HEAD_DIM), BF16),
        compiler_params=_cparams(("parallel", "parallel", "arbitrary"), 48),
        name="flash_prompt",
    )(q, k, v)


def _qlat_kernel(qn_ref, qr_ref, w_ref, o_ref):
    o_ref[:, :KV_LORA] = jnp.dot(qn_ref[...], w_ref[0], preferred_element_type=F32).astype(BF16)
    o_ref[:, KV_LORA:] = qr_ref[...]


def _q_latent(q, w_uk_t):
    nb = q.shape[0]
    return pl.pallas_call(
        _qlat_kernel,
        grid=(N_MLA_HEADS,),
        in_specs=[pl.BlockSpec((nb, QK_NOPE_DIM), lambda h: (0, 2 * h)),
                  pl.BlockSpec((nb, LANE), lambda h: (0, 2 * h + 1)),
                  pl.BlockSpec((1, QK_NOPE_DIM, KV_LORA), lambda h: (h, 0, 0))],
        out_specs=pl.BlockSpec((nb, QLAT_W), lambda h: (0, h)),
        out_shape=jax.ShapeDtypeStruct((nb, N_MLA_HEADS * QLAT_W), BF16),
        compiler_params=_cparams(("parallel",), 32),
        name="q_latent",
    )(q, q, w_uk_t)


def _vup_kernel(o_ref, w_ref, out_ref):
    out_ref[...] = jnp.dot(o_ref[...], w_ref[0], preferred_element_type=F32).astype(BF16)


def _v_up(o_lat, w_uv_h):
    nb = o_lat.shape[0]
    return pl.pallas_call(
        _vup_kernel,
        grid=(N_MLA_HEADS,),
        in_specs=[pl.BlockSpec((nb, KV_LORA), lambda h: (0, h)),
                  pl.BlockSpec((1, KV_LORA, V_HEAD_DIM), lambda h: (h, 0, 0))],
        out_specs=pl.BlockSpec((nb, V_HEAD_DIM), lambda h: (0, h)),
        out_shape=jax.ShapeDtypeStruct((nb, N_MLA_HEADS * V_HEAD_DIM), BF16),
        compiler_params=_cparams(("parallel",), 32),
        name="v_up",
    )(o_lat, w_uv_h)


def _page_copies(pt_ref, ckv_hbm, kr_hbm, cbuf, rbuf, sems, first_page, slot, pages):
    copies = []
    for j in range(pages):
        pg = pt_ref[first_page + j]
        copies.append(pltpu.make_async_copy(ckv_hbm.at[0, pg], cbuf.at[slot, j], sems.at[0, slot]))
        copies.append(pltpu.make_async_copy(kr_hbm.at[0, pg], rbuf.at[slot, j], sems.at[1, slot]))
    return copies


def _decode_kernel(pt_ref, q_ref, cnew_ref, rnew_ref, ckv_hbm, kr_hbm, o_ref,
                   cbuf, rbuf, sems, cb_scr, rb_scr, m_scr, l_scr, acc_scr, *, pages):
    s = pl.program_id(1)
    ns = pl.num_programs(1)
    step = pl.program_id(0) * ns + s
    slot = lax.rem(step, 2)
    fetch = functools.partial(_page_copies, pt_ref, ckv_hbm, kr_hbm, cbuf, rbuf, sems, pages=pages)

    @pl.when(step == 0)
    def _():
        for cp in fetch(0, 0):
            cp.start()

    @pl.when(step + 1 < pl.num_programs(0) * ns)
    def _():
        for cp in fetch((step + 1) * pages, 1 - slot):
            cp.start()

    for cp in fetch(step * pages, slot):
        cp.wait()

    @pl.when(s == 0)
    def _():
        m_scr[...] = jnp.full_like(m_scr, NEG_BIG)
        l_scr[...] = jnp.zeros_like(l_scr)
        acc_scr[...] = jnp.zeros_like(acc_scr)

    q = q_ref[0]
    ql = q[:, :KV_LORA]
    qr = q[:, KV_LORA:KV_LORA + QK_ROPE_DIM]
    def cast_page(j, carry):
        off = pl.multiple_of(j * PAGE_SIZE, PAGE_SIZE)
        cb_scr[pl.ds(off, PAGE_SIZE), :] = cbuf[slot, j].astype(BF16)
        return carry

    lax.fori_loop(0, pages, cast_page, 0)
    for j in range(pages):
        rb_scr[:, j * PAGE_SIZE:(j + 1) * PAGE_SIZE] = rbuf[slot, j].astype(BF16)
    cb = cb_scr[...]
    sc = (_dot_nt(ql, cb) + jnp.dot(qr, rb_scr[...], preferred_element_type=F32)) * ATTN_SCALE
    m_old = m_scr[...]
    m_new = jnp.maximum(m_old, jnp.max(sc, axis=1, keepdims=True))
    alpha = jnp.exp(m_old - m_new)
    p = jnp.exp(sc - m_new)
    l_scr[...] = alpha * l_scr[...] + jnp.sum(p, axis=1, keepdims=True)
    pv = jnp.dot(p.astype(BF16), cb, preferred_element_type=F32)
    acc_scr[...] = alpha * acc_scr[...] + pv
    m_scr[...] = m_new

    @pl.when(s == pl.num_programs(1) - 1)
    def _():
        cn = cnew_ref[0].astype(BF16).astype(F32)
        rn = rnew_ref[0].astype(BF16).astype(F32)
        s_new = (jnp.sum(ql.astype(F32) * cn, axis=1, keepdims=True)
                 + jnp.sum(qr.astype(F32) * rn, axis=1, keepdims=True)) * ATTN_SCALE
        m_old = m_scr[...]
        m_fin = jnp.maximum(m_old, s_new)
        alpha = jnp.exp(m_old - m_fin)
        p_new = jnp.exp(s_new - m_fin)
        l_fin = alpha * l_scr[...] + p_new
        acc = alpha * acc_scr[...] + p_new.astype(BF16).astype(F32) * cn
        o_ref[0] = (acc / l_fin).astype(BF16)


def _decode(q_cat, ckv_new, kr_new, cache_ckv, cache_krope_t, page_table, pages):
    nb, n_pages = page_table.shape
    steps = n_pages // pages

    grid_spec = pltpu.PrefetchScalarGridSpec(
        num_scalar_prefetch=1,
        grid=(nb, steps),
        in_specs=[pl.BlockSpec((1, N_MLA_HEADS, QLAT_W), lambda b, s, pt: (b, 0, 0)),
                  pl.BlockSpec((1, 1, KV_LORA), lambda b, s, pt: (b, 0, 0)),
                  pl.BlockSpec((1, 1, QK_ROPE_DIM), lambda b, s, pt: (b, 0, 0)),
                  pl.BlockSpec(memory_space=pl.ANY), pl.BlockSpec(memory_space=pl.ANY)],
        out_specs=pl.BlockSpec((1, N_MLA_HEADS, KV_LORA), lambda b, s, pt: (b, 0, 0)),
        scratch_shapes=[pltpu.VMEM((2, pages, PAGE_SIZE, KV_LORA), F32),
                        pltpu.VMEM((2, pages, QK_ROPE_DIM, PAGE_SIZE), F32),
                        pltpu.SemaphoreType.DMA((2, 2)),
                        pltpu.VMEM((pages * PAGE_SIZE, KV_LORA), BF16),
                        pltpu.VMEM((QK_ROPE_DIM, pages * PAGE_SIZE), BF16),
                        pltpu.VMEM((N_MLA_HEADS, 1), F32), pltpu.VMEM((N_MLA_HEADS, 1), F32),
                        pltpu.VMEM((N_MLA_HEADS, KV_LORA), F32)],
    )
    return pl.pallas_call(
        functools.partial(_decode_kernel, pages=pages),
        grid_spec=grid_spec,
        out_shape=jax.ShapeDtypeStruct((nb, N_MLA_HEADS, KV_LORA), BF16),
        compiler_params=_cparams(("arbitrary", "arbitrary"), 48),
        name="decode_attn",
    )(page_table.reshape(-1), q_cat, ckv_new, kr_new, cache_ckv, cache_krope_t)


def _layer_norm(x, g, b):
    mu = jnp.mean(x, axis=1, keepdims=True)
    xc = x - mu
    var = jnp.mean(xc * xc, axis=1, keepdims=True)
    return xc * lax.rsqrt(var + NORM_EPS) * g + b


def _outproj_kernel(*refs, tn, aliased):
    if aliased:
        refs = refs[1:]
    ssd_ref, mla_ref, wt_ref, wb_ref, x_ref, g_ref, b_ref, o_ref = refs
    j = pl.program_id(1)
    mix = (jnp.dot(ssd_ref[...], wt_ref[...], preferred_element_type=F32)
           + jnp.dot(mla_ref[...], wb_ref[...], preferred_element_type=F32))
    val = DEEPNORM_ALPHA * x_ref[...] + mix
    nj = D_MODEL // tn
    for jj in range(nj):
        @pl.when(j == jj)
        def _(jj=jj):
            o_ref[:, jj * tn:(jj + 1) * tn] = val

    @pl.when(j == nj - 1)
    def _():
        o_ref[...] = _layer_norm(o_ref[...], g_ref[...], b_ref[...])


def _outproj_ln(ssd, mla, w_out_b, x, g, b, tm, total_rows, row_block0, buf=None):
    m = x.shape[0]
    tn = 512
    aliased = buf is not None
    in_specs = [pl.BlockSpec((tm, D_SSD), lambda i, j: (i, 0)),
                pl.BlockSpec((tm, D_SSD), lambda i, j: (i, 0)),
                pl.BlockSpec((D_SSD, tn), lambda i, j: (0, j)),
                pl.BlockSpec((D_SSD, tn), lambda i, j: (1, j)),
                pl.BlockSpec((tm, tn), lambda i, j: (i, j)),
                pl.BlockSpec((1, D_MODEL), lambda i, j: (0, 0)),
                pl.BlockSpec((1, D_MODEL), lambda i, j: (0, 0))]
    args = [ssd, mla, w_out_b, w_out_b, x, g, b]
    if aliased:
        in_specs = [pl.BlockSpec(memory_space=pl.ANY)] + in_specs
        args = [buf] + args
    return pl.pallas_call(
        functools.partial(_outproj_kernel, tn=tn, aliased=aliased),
        grid=(m // tm, D_MODEL // tn),
        in_specs=in_specs,
        out_specs=pl.BlockSpec((tm, D_MODEL), lambda i, j: (i + row_block0, 0)),
        out_shape=jax.ShapeDtypeStruct((total_rows, D_MODEL), F32),
        input_output_aliases={0: 0} if aliased else {},
        compiler_params=_cparams(("parallel", "arbitrary"), 56),
        name="out_proj_ln",
    )(*args)


def _router_kernel(h_ref, wh_ref, wm_ref, bias_ref, ids_ref, gates_ref):
    hh, hm, _ = _split3(h_ref[...])
    wh = wh_ref[...]
    d = functools.partial(jnp.dot, preferred_element_type=F32)
    lg = d(hh, wh) + (d(hh, wm_ref[...]) + d(hm, wh)) + bias_ref[...]
    lane = lax.broadcasted_iota(jnp.int32, lg.shape, 1)
    big = jnp.int32(1 << 20)
    gmask = lane < N_EXPERT_GROUPS
    gl = jnp.where(gmask, lg, -jnp.inf)
    ge = jnp.exp(gl - jnp.max(gl, axis=1, keepdims=True))
    gp = ge / jnp.sum(ge, axis=1, keepdims=True)
    g_val = jnp.max(gp, axis=1, keepdims=True)
    g_idx = jnp.min(jnp.where(gmask & (gp == g_val), lane, big), axis=1, keepdims=True)
    eid = lane - N_EXPERT_GROUPS
    emask = (eid >= 0) & (eid < N_EXPERTS) & (lax.shift_right_arithmetic(eid, 3) == g_idx)
    el = jnp.where(emask, lg, -jnp.inf)
    ee = jnp.exp(el - jnp.max(el, axis=1, keepdims=True))
    ep = jnp.where(emask, ee / jnp.sum(ee, axis=1, keepdims=True), -1.0)
    v1 = jnp.max(ep, axis=1, keepdims=True)
    i1 = jnp.min(jnp.where(emask & (ep == v1), lane, big), axis=1, keepdims=True)
    ep2 = jnp.where(lane == i1, -1.0, ep)
    v2 = jnp.max(ep2, axis=1, keepdims=True)
    i2 = jnp.min(jnp.where(emask & (lane != i1) & (ep2 == v2), lane, big), axis=1, keepdims=True)
    tot = v1 + v2
    ids_ref[...] = jnp.where(lane == 0, i1 - N_EXPERT_GROUPS,
                             jnp.where(lane == 1, i2 - N_EXPERT_GROUPS, 0))
    gates_ref[...] = jnp.where(lane == 0, g_val * (v1 / tot), jnp.where(lane == 1, g_val * (v2 / tot), 0.0))


def _router(h, wh, wm, bias, tm):
    t = h.shape[0]
    const = lambda shape: pl.BlockSpec(shape, lambda i: (0, 0))
    return pl.pallas_call(
        _router_kernel,
        grid=(t // tm,),
        in_specs=[pl.BlockSpec((tm, D_MODEL), lambda i: (i, 0)),
                  const((D_MODEL, LANE)), const((D_MODEL, LANE)), const((1, LANE))],
        out_specs=[pl.BlockSpec((tm, LANE), lambda i: (i, 0)), pl.BlockSpec((tm, LANE), lambda i: (i, 0))],
        out_shape=[jax.ShapeDtypeStruct((t, LANE), jnp.int32), jax.ShapeDtypeStruct((t, LANE), F32)],
        compiler_params=_cparams(("parallel",), 48),
        name="router",
    )(h, wh, wm, bias)


def _row_copy(src, src_row, dst, dst_row, sem):
    return pltpu.make_async_copy(src.at[pl.ds(src_row, 1)], dst.at[pl.ds(dst_row, 1)], sem)


def _for_rows(count, fn):
    full = count // ROW_UNROLL

    def body_unrolled(c, carry):
        for u in range(ROW_UNROLL):
            fn(c * ROW_UNROLL + u)
        return carry

    def body_one(r, carry):
        fn(r)
        return carry

    lax.fori_loop(0, full, body_unrolled, 0)
    lax.fori_loop(full * ROW_UNROLL, count, body_one, 0)


def _moe_kernel(blk_e, blk_start, blk_n, n_live, tok, dst, h_hbm, wg_ref, wu_ref, wd_ref, y_hbm,
                xg, xb, acc, sem_in, sem_out):
    i = pl.program_id(0)
    n = pl.program_id(1)
    last_i = pl.num_programs(0) - 1
    last_n = pl.num_programs(1) - 1
    nrows = blk_n[i]

    def start_gather(blk):
        base = blk_start[blk]
        _for_rows(blk_n[blk], lambda r: _row_copy(h_hbm, tok[base + r], xg, r, sem_in).start())

    def start_scatter(blk):
        base = blk_start[blk]
        _for_rows(blk_n[blk], lambda r: _row_copy(acc, r, y_hbm, dst[base + r], sem_out).start())

    def wait_rows(src, dst_ref, sem, count):
        _for_rows(count, lambda r: _row_copy(src, 0, dst_ref, 0, sem).wait())

    @pl.when((i == 0) & (n == 0))
    def _():
        xg[...] = jnp.zeros_like(xg)
        acc[...] = jnp.zeros_like(acc)
        start_gather(0)

    @pl.when(n == 0)
    def _():
        wait_rows(h_hbm, xg, sem_in, nrows)

        @pl.when(i > 0)
        def _():
            wait_rows(acc, y_hbm, sem_out, blk_n[jnp.maximum(i - 1, 0)])

    for k in range(1, MOE_BLK // MOE_SUB + 1):
        rows = k * MOE_SUB

        @pl.when((nrows > rows - MOE_SUB) & (nrows <= rows))
        def _(rows=rows):
            @pl.when(n == 0)
            def _():
                xb[0:rows, :] = xg[0:rows, :].astype(BF16)

            x = xb[0:rows, :]
            hid = (_silu(jnp.dot(x, wg_ref[...].astype(BF16), preferred_element_type=F32))
                   * jnp.dot(x, wu_ref[...].astype(BF16), preferred_element_type=F32)).astype(BF16)
            wd = wd_ref[...].astype(BF16)
            for c in range(D_MODEL // MOE_DOWN_TN):
                cs = slice(c * MOE_DOWN_TN, (c + 1) * MOE_DOWN_TN)
                part = jnp.dot(hid, wd[:, cs], preferred_element_type=F32)
                acc[0:rows, cs] = jnp.where(n > 0, acc[0:rows, cs], 0.0) + part

    @pl.when((n == 0) & (i < last_i))
    def _():
        start_gather(i + 1)

    @pl.when(n == last_n)
    def _():
        start_scatter(i)

        @pl.when(i == last_i)
        def _():
            wait_rows(acc, y_hbm, sem_out, nrows)


def _moe(h_all, blk_e, blk_start, blk_n, n_live, tok, dst, w_gate_e, w_up_e, w_down_e, nb_max):
    t = h_all.shape[0]
    nt = D_EXPERT // MOE_TN

    def live_n(i, n, nl):
        return jnp.where(i < nl[0], n, nt - 1)

    grid_spec = pltpu.PrefetchScalarGridSpec(
        num_scalar_prefetch=6,
        grid=(nb_max, nt),
        in_specs=[pl.BlockSpec(memory_space=pl.ANY),
                  pl.BlockSpec((None, None, D_MODEL, MOE_TN),
                               lambda i, n, be, bs, bn, nl, tk, ds: (0, be[i], 0, live_n(i, n, nl))),
                  pl.BlockSpec((None, None, D_MODEL, MOE_TN),
                               lambda i, n, be, bs, bn, nl, tk, ds: (0, be[i], 0, live_n(i, n, nl))),
                  pl.BlockSpec((None, None, MOE_TN, D_MODEL),
                               lambda i, n, be, bs, bn, nl, tk, ds: (0, be[i], live_n(i, n, nl), 0))],
        out_specs=pl.BlockSpec(memory_space=pl.ANY),
        scratch_shapes=[pltpu.VMEM((MOE_BLK, D_MODEL), F32), pltpu.VMEM((MOE_BLK, D_MODEL), BF16),
                        pltpu.VMEM((MOE_BLK, D_MODEL), F32),
                        pltpu.SemaphoreType.DMA(()), pltpu.SemaphoreType.DMA(())],
    )
    return pl.pallas_call(
        _moe_kernel,
        grid_spec=grid_spec,
        out_shape=jax.ShapeDtypeStruct((TOP_K * t, D_MODEL), F32),
        compiler_params=_cparams(("arbitrary", "arbitrary"), 56),
        name="moe_experts",
    )(blk_e, blk_start, blk_n, n_live, tok, dst, h_all, w_gate_e, w_up_e, w_down_e)


def _dispatch_plan(ids, nb_max):
    t = ids.shape[0]
    n_assign = TOP_K * t
    e_flat = ids.reshape(-1)
    order = jnp.argsort(e_flat).astype(jnp.int32)
    counts = jnp.bincount(e_flat, length=N_EXPERTS).astype(jnp.int32)
    start = jnp.cumsum(counts) - counts
    nblk = (counts + MOE_BLK - 1) // MOE_BLK
    blk_end = jnp.cumsum(nblk)
    n_live = blk_end[-1]
    bi = jnp.arange(nb_max, dtype=jnp.int32)
    live = bi < n_live
    be = jnp.minimum(jnp.searchsorted(blk_end, bi, side='right'), N_EXPERTS - 1).astype(jnp.int32)
    last_e = jnp.max(jnp.where(counts > 0, jnp.arange(N_EXPERTS, dtype=jnp.int32), 0))
    be = jnp.where(live, be, last_e)
    k = bi - (blk_end[be] - nblk[be])
    bstart = jnp.where(live, start[be] + k * MOE_BLK, 0).astype(jnp.int32)
    bn = jnp.where(live, jnp.clip(counts[be] - k * MOE_BLK, 0, MOE_BLK), 0).astype(jnp.int32)
    tok = order // TOP_K
    dst = (order % TOP_K) * t + tok
    return be, bstart, bn, n_live.reshape(1).astype(jnp.int32), tok, dst


def _ln2_kernel(h_ref, y0_ref, y1_ref, gt_ref, g_ref, b_ref, op_ref, os_ref, *, n_prompt_blocks):
    i = pl.program_id(0)
    gt = gt_ref[...]
    ffn = gt[:, 0:1] * y0_ref[...] + gt[:, 1:2] * y1_ref[...]
    out = _layer_norm(DEEPNORM_ALPHA * h_ref[...] + ffn, g_ref[...], b_ref[...])

    @pl.when(i < n_prompt_blocks)
    def _():
        op_ref[...] = out

    @pl.when(i >= n_prompt_blocks)
    def _():
        os_ref[...] = out


def _ln2(h_all, y2, gates, g, b, n_prompt, tm):
    t = h_all.shape[0]
    nblk = t // tm
    npb = n_prompt // tm
    const = lambda shape: pl.BlockSpec(shape, lambda i: (0, 0))
    return pl.pallas_call(
        functools.partial(_ln2_kernel, n_prompt_blocks=npb),
        grid=(nblk,),
        in_specs=[pl.BlockSpec((tm, D_MODEL), lambda i: (i, 0)),
                  pl.BlockSpec((tm, D_MODEL), lambda i: (i, 0)),
                  pl.BlockSpec((tm, D_MODEL), lambda i: (i + nblk, 0)),
                  pl.BlockSpec((tm, LANE), lambda i: (i, 0)),
                  const((1, D_MODEL)), const((1, D_MODEL))],
        out_specs=[pl.BlockSpec((tm, D_MODEL), lambda i: (jnp.minimum(i, npb - 1), 0)),
                   pl.BlockSpec((tm, D_MODEL), lambda i: (jnp.maximum(i - npb, 0), 0))],
        out_shape=[jax.ShapeDtypeStruct((n_prompt, D_MODEL), F32),
                   jax.ShapeDtypeStruct((t - n_prompt, D_MODEL), F32)],
        compiler_params=_cparams(("arbitrary",), 48),
        name="ffn_ln",
    )(h_all, y2, y2, gates, g, b)


def _rot_cols(w):
    half = w.shape[-1] // 2
    return jnp.concatenate([-w[..., half:], w[..., :half]], axis=-1)


def _rope_table(pos):
    half = QK_ROPE_DIM // 2
    inv_freq = ROPE_THETA ** (-jnp.arange(half, dtype=F32) / half)
    ang = pos.astype(F32)[:, None] * inv_freq
    cos, sin = jnp.cos(ang), jnp.sin(ang)
    return jnp.concatenate([cos, cos, sin, sin], axis=1)


def _expander(width):
    k = lax.broadcasted_iota(jnp.int32, (LANE, N_SSD_HEADS * width), 0)
    c = lax.broadcasted_iota(jnp.int32, (LANE, N_SSD_HEADS * width), 1)
    return (k == c // width).astype(BF16)


def _row_tile(m, pref):
    while m % pref:
        pref //= 2
    return pref


def kernel(x_prompt, x_sample, cache_ckv, cache_krope, state_ssm, state_conv, page_table, w_in, conv_w, conv_b, dt_bias, a_log, d_skip, ssd_norm_w, q_norm_w, w_uq, kv_norm_w, w_uk, w_uv, w_out, ln1_g, ln1_b, w_router_group, b_router_group, w_router_expert, b_router_expert, w_gate_e, w_up_e, w_down_e, ln2_g, ln2_b):
    b, t, _ = x_prompt.shape
    nb, dec_t, _ = x_sample.shape
    assert dec_t == 1 and w_in.shape[0] == 1
    n_pages = page_table.shape[1]
    past = n_pages * PAGE_SIZE
    n_prompt = b * t
    n_tok = n_prompt + nb

    wt = jnp.swapaxes(w_in[0], 0, 1)
    s0 = D_SSD
    s1 = s0 + CONV_CH
    s2 = s1 + N_SSD_HEADS
    s3 = s2 + Q_LORA
    s4 = s3 + KV_LORA
    w_kr_t = wt[s4:]
    half = QK_ROPE_DIM // 2
    w_all_t = jnp.concatenate(
        [wt[s0:s1], wt[s2:s3], wt[:s0], wt[s3:s4], w_kr_t, -w_kr_t[half:], w_kr_t[:half], wt[s1:s2],
         jnp.zeros((LANE - N_SSD_HEADS, D_MODEL), F32)], axis=0).astype(BF16)
    pad_heads = lambda v: jnp.pad(v, (0, LANE - N_SSD_HEADS)).reshape(1, LANE)
    dtb = pad_heads(dt_bias[0])
    a_neg = pad_heads(-jnp.exp(a_log[0].astype(F32)))
    dsk = jnp.repeat(d_skip[0], SSD_HEAD_DIM).reshape(1, D_SSD)
    nw = ssd_norm_w[0].reshape(1, D_SSD)
    cw = conv_w[0]
    cb = conv_b[0].reshape(1, CONV_CH)
    e64 = _expander(SSD_HEAD_DIM)
    e128 = _expander(LANE)
    wq = w_uq[0]
    wq_r = wq[..., QK_NOPE_DIM:]
    wq_all = jnp.concatenate([wq[..., :QK_NOPE_DIM], wq_r, _rot_cols(wq_r)], axis=-1)
    wq_all = wq_all.reshape(Q_LORA, N_MLA_HEADS * QK_PAD).astype(BF16)
    wkv = jnp.concatenate([w_uk[0].reshape(KV_LORA, -1), w_uv[0].reshape(KV_LORA, -1)], axis=1).astype(BF16)
    w_uk_t = jnp.transpose(w_uk[0], (1, 2, 0)).astype(BF16)
    w_uv_h = jnp.transpose(w_uv[0], (1, 0, 2)).astype(BF16)
    w_out_b = w_out[0].astype(BF16)
    qnw = q_norm_w[0].reshape(1, Q_LORA)
    kvnw = kv_norm_w[0].reshape(1, KV_LORA)
    g1, b1 = ln1_g[0].reshape(1, D_MODEL), ln1_b[0].reshape(1, D_MODEL)
    g2, b2 = ln2_g[0].reshape(1, D_MODEL), ln2_b[0].reshape(1, D_MODEL)
    w_r = jnp.concatenate([w_router_group[0], w_router_expert[0],
                           jnp.zeros((D_MODEL, LANE - N_EXPERT_GROUPS - N_EXPERTS), F32)], axis=1)
    w_r_hi = w_r.astype(BF16)
    w_r_mid = (w_r - w_r_hi.astype(F32)).astype(BF16)
    b_r = jnp.concatenate([b_router_group[0], b_router_expert[0],
                           jnp.zeros((LANE - N_EXPERT_GROUPS - N_EXPERTS,), F32)]).reshape(1, LANE)

    xp = x_prompt.reshape(n_prompt, D_MODEL)
    xsm = x_sample.reshape(nb, D_MODEL)

    proj_p = _matmul(xp, w_all_t, _row_tile(n_prompt, 512), 768)
    ssd_p, ssm_p, conv_p = _ssd_prompt(proj_p, b, t, cw, cb, dtb, a_neg, dsk, nw, e64, e128)
    tm_p = _row_tile(t, 256)
    tab_p = _rope_table(jnp.arange(t))
    q_p, ckv_p, kr_p, k_p, v_p = _mla_prep(proj_p, tab_p, qnw, kvnw, wq_all, wkv, tm_p, True)
    mla_p = _flash(q_p, k_p, v_p, b, t, _row_tile(t, 512))
    tm_o = _row_tile(n_prompt, 512)
    h_all = _outproj_ln(ssd_p, mla_p, w_out_b, xp, g1, b1, tm_o, n_tok, 0,
                        buf=jnp.zeros((n_tok, D_MODEL), F32))

    proj_s = _matmul(xsm, w_all_t, nb, 768)
    conv_buf_t = jnp.transpose(state_conv[0], (1, 0, 2))
    ssd_s, ssm_s, conv_s_t = _ssd_sample(proj_s, conv_buf_t, state_ssm[0], cw, cb, dtb, a_neg, dsk, nw, e64)
    tab_s = _rope_table(jnp.full((nb,), past, jnp.int32))
    q_s, ckv_s, kr_s = _mla_prep(proj_s, tab_s, qnw, kvnw, wq_all, wkv, nb, False)
    q_cat = _q_latent(q_s, w_uk_t).reshape(nb, N_MLA_HEADS, QLAT_W)
    pages = math.gcd(n_pages, DECODE_PAGES)
    o_lat = _decode(q_cat, ckv_s.reshape(nb, 1, KV_LORA), kr_s.reshape(nb, 1, QK_ROPE_DIM),
                    cache_ckv, jnp.swapaxes(cache_krope, 2, 3), page_table, pages)
    mla_s = _v_up(o_lat.reshape(nb, N_MLA_HEADS * KV_LORA), w_uv_h)
    h_all = _outproj_ln(ssd_s, mla_s, w_out_b, xsm, g1, b1, nb, n_tok, n_prompt // nb, buf=h_all)

    tm_r = _row_tile(math.gcd(n_prompt, nb), 128)
    ids, gates = _router(h_all, w_r_hi, w_r_mid, b_r, _row_tile(n_tok, 640))
    nb_max = (TOP_K * n_tok) // MOE_BLK + N_EXPERTS
    be, bstart, bn, n_live, tok, dst = _dispatch_plan(ids[:, :TOP_K], nb_max)
    y2 = _moe(h_all, be, bstart, bn, n_live, tok, dst, w_gate_e, w_up_e, w_down_e, nb_max)
    y_p, y_s = _ln2(h_all, y2, gates, g2, b2, n_prompt, tm_r)

    return (y_p.reshape(b, t, D_MODEL), y_s.reshape(nb, 1, D_MODEL),
            ckv_p.reshape(1, b, t, KV_LORA), kr_p.reshape(1, b, t, QK_ROPE_DIM),
            ssm_p[None], conv_p[None],
            ckv_s.reshape(1, nb, 1, KV_LORA), kr_s.reshape(1, nb, 1, QK_ROPE_DIM),
            ssm_s[None], jnp.transpose(conv_s_t, (1, 0, 2))[None])
```

```python
import functools
import math

import jax
import jax.numpy as jnp
from jax import lax
from jax.experimental import pallas as pl
from jax.experimental.pallas import tpu as pltpu

F32 = jnp.float32
BF16 = jnp.bfloat16

D_MODEL = 4096
D_SSD = 2048
SSD_HEAD_DIM = 64
N_SSD_HEADS = 32
N_SSD_GROUPS = 4
GROUP_W = D_SSD // N_SSD_GROUPS
D_STATE = 128
CONV_W = 4
CONV_CH = D_SSD + 2 * N_SSD_GROUPS * D_STATE
SSD_CHUNK = 128
N_MLA_HEADS = 16
V_HEAD_DIM = 128
QK_NOPE_DIM = 128
QK_ROPE_DIM = 64
Q_LORA = 1024
KV_LORA = 512
ROPE_THETA = 10000.0
ATTN_SCALE = (QK_NOPE_DIM + QK_ROPE_DIM) ** -0.5
PAGE_SIZE = 128
N_EXPERT_GROUPS = 8
EXPERTS_PER_GROUP = 8
N_EXPERTS = 64
TOP_K = 2
D_EXPERT = 1024
DEPTH = 1
DEEPNORM_ALPHA = (2 * DEPTH) ** 0.25
NORM_EPS = 1e-5

LANE = 128
SUBLANE = 8
QK_PAD = 256
QLAT_W = 640

C_XBC, C_CQ, C_Z, C_CKV, C_KR, C_DT = 0, 3072, 4096, 6144, 6656, 6784
PROJ_W = 6912

FLASH_HEADS = 2
DECODE_PAGES = 32
DECODE_GROUP = 8
MOE_BLK = 512
MOE_SUB = 64
MOE_TN = 256
MOE_DOWN_TN = 1024
ROW_UNROLL = 8
NEG_BIG = -1e30
LOG2_E = 1.4426950408889634


def _cparams(sem, vmem_mb):
    return pltpu.CompilerParams(dimension_semantics=sem, vmem_limit_bytes=vmem_mb * 1024 * 1024)


def _sigmoid(x):
    return 1.0 / (1.0 + jnp.exp(-x))


def _silu(x):
    return x * _sigmoid(x)


def _split3(x):
    hi = x.astype(BF16)
    r1 = x - hi.astype(F32)
    mid = r1.astype(BF16)
    lo = (r1 - mid.astype(F32)).astype(BF16)
    return hi, mid, lo


def _expand(x, e):
    hi, mid, lo = _split3(x)
    d = functools.partial(jnp.dot, preferred_element_type=F32)
    return d(hi, e) + d(mid, e) + d(lo, e)


def _dot_nt(a, b):
    return lax.dot_general(a, b, (((1,), (1,)), ((), ())), preferred_element_type=F32)


def _dot_tn(a, b):
    return lax.dot_general(a, b, (((0,), (0,)), ((), ())), preferred_element_type=F32)


def _mm_kernel(x_ref, w_ref, o_ref, xb_ref):
    @pl.when(pl.program_id(1) == 0)
    def _():
        xb_ref[...] = x_ref[...].astype(BF16)

    o_ref[...] = _dot_nt(xb_ref[...], w_ref[...])


def _matmul(x, w_t, tm, tn):
    m, k = x.shape
    n = w_t.shape[0]
    return pl.pallas_call(
        _mm_kernel,
        grid=(m // tm, n // tn),
        in_specs=[pl.BlockSpec((tm, k), lambda i, j: (i, 0)),
                  pl.BlockSpec((tn, k), lambda i, j: (j, 0))],
        out_specs=pl.BlockSpec((tm, tn), lambda i, j: (i, j)),
        out_shape=jax.ShapeDtypeStruct((m, n), F32),
        scratch_shapes=[pltpu.VMEM((tm, k), BF16)],
        compiler_params=_cparams(("parallel", "arbitrary"), 48),
        name="in_proj",
    )(x, w_t)


def _softplus(x):
    return jnp.maximum(x, 0.0) + jnp.log1p(jnp.exp(-jnp.abs(x)))


def _gated_norm(y, xs, z, dsk, nw):
    y = (y + dsk * xs) * _silu(z)
    outs = []
    for g in range(N_SSD_GROUPS):
        yg = y[:, g * GROUP_W:(g + 1) * GROUP_W]
        ms = jnp.mean(yg * yg, axis=1, keepdims=True)
        outs.append(yg * lax.rsqrt(ms + NORM_EPS))
    return jnp.concatenate(outs, axis=1) * nw


def _cumsum_rows(x):
    row = lax.broadcasted_iota(jnp.int32, x.shape, 0)
    s = 1
    while s < x.shape[0]:
        x = x + jnp.where(row >= s, pltpu.roll(x, s, 0), 0.0)
        s *= 2
    return x


def _ssd_prompt_kernel(xbc_ref, z_ref, dt_ref, cw_ref, cb_ref, dtb_ref, a_ref, dsk_ref, nw_ref,
                       e64_ref, e128_ref, out_ref, st_ref, conv_ref, xs_scr, st_scr):
    c = pl.program_id(1)
    q = SSD_CHUNK

    @pl.when(c == 0)
    def _():
        xs_scr[0:SUBLANE, :] = jnp.zeros((SUBLANE, CONV_CH), F32)
        st_scr[...] = jnp.zeros_like(st_scr)

    xs_scr[SUBLANE:SUBLANE + q, :] = xbc_ref[...]
    acc = cb_ref[...] + cw_ref[CONV_W - 1:CONV_W, :] * xs_scr[SUBLANE:SUBLANE + q, :]
    for k in range(CONV_W - 1):
        off = SUBLANE - (CONV_W - 1) + k
        acc = acc + cw_ref[k:k + 1, :] * xs_scr[off:off + q, :]
    xbc = _silu(acc)
    conv_ref[0] = xs_scr[q + SUBLANE - (CONV_W - 1):q + SUBLANE, :]
    xs_scr[0:SUBLANE, :] = xs_scr[q:q + SUBLANE, :]

    xs = xbc[:, :D_SSD]
    bm = xbc[:, D_SSD:D_SSD + N_SSD_GROUPS * D_STATE]
    cm = xbc[:, D_SSD + N_SSD_GROUPS * D_STATE:]

    lane = lax.broadcasted_iota(jnp.int32, (q, LANE), 1)
    row = lax.broadcasted_iota(jnp.int32, (q, LANE), 0)
    dt = jnp.where(lane < N_SSD_HEADS, _softplus(dt_ref[...] + dtb_ref[...]), 0.0)
    a_cs = _cumsum_rows(dt * a_ref[...])
    a_cs_t = a_cs.T
    e64 = e64_ref[...]
    dt_e = _expand(dt, e64)
    acs_e = _expand(a_cs, e64)
    acs_b = _expand(a_cs, e128_ref[...])
    xdt = xs * dt_e
    last = acs_e[q - 1:q, :]
    xw = (xdt * jnp.exp(last - acs_e)).astype(BF16)
    xdt_b = xdt.astype(BF16)
    chunk_decay = jnp.exp(last)
    in_decay = jnp.exp(acs_e)
    causal = row >= lane

    ys = []
    for g in range(N_SSD_GROUPS):
        gs = slice(g * GROUP_W, (g + 1) * GROUP_W)
        bg = bm[:, g * D_STATE:(g + 1) * D_STATE].astype(BF16)
        cg = cm[:, g * D_STATE:(g + 1) * D_STATE].astype(BF16)
        cb = _dot_nt(cg, bg)
        st_g = st_scr[:, gs]
        y_off = jnp.dot(cg, st_g.astype(BF16), preferred_element_type=F32) * in_decay[:, gs]
        yd = []
        for pr in range(GROUP_W // LANE):
            h0 = g * (N_SSD_HEADS // N_SSD_GROUPS) + 2 * pr
            ms = []
            for h in (h0, h0 + 1):
                seg = acs_b[:, h * LANE:(h + 1) * LANE] - a_cs_t[h:h + 1, :]
                dec = jnp.exp(jnp.where(causal, seg, -jnp.inf))
                ms.append((cb * dec).astype(BF16))
            lhs = jnp.concatenate(ms, axis=1)
            xp = xdt_b[:, h0 * SSD_HEAD_DIM:(h0 + 2) * SSD_HEAD_DIM]
            zero = jnp.zeros_like(xp)
            rhs = jnp.concatenate([jnp.where(lane < SSD_HEAD_DIM, xp, zero),
                                   jnp.where(lane >= SSD_HEAD_DIM, xp, zero)], axis=0)
            yd.append(jnp.dot(lhs, rhs, preferred_element_type=F32))
        ys.append(jnp.concatenate(yd, axis=1) + y_off)
        st_scr[:, gs] = st_g * chunk_decay[:, gs] + _dot_tn(bg, xw[:, gs])
    y = jnp.concatenate(ys, axis=1)

    out_ref[...] = _gated_norm(y, xs, z_ref[...], dsk_ref[...], nw_ref[...]).astype(BF16)

    @pl.when(c == pl.num_programs(1) - 1)
    def _():
        st_ref[0] = st_scr[...].T.reshape(N_SSD_HEADS, SSD_HEAD_DIM, D_STATE)


def _ssd_prompt(proj, b, t, cw, cb, dtb, a, dsk, nw, e64, e128):
    nc = t // SSD_CHUNK
    q = SSD_CHUNK
    const = lambda shape: pl.BlockSpec(shape, lambda i, j: (0, 0))
    return pl.pallas_call(
        _ssd_prompt_kernel,
        grid=(b, nc),
        in_specs=[pl.BlockSpec((q, CONV_CH), lambda i, j: (i * nc + j, C_XBC // CONV_CH)),
                  pl.BlockSpec((q, D_SSD), lambda i, j: (i * nc + j, C_Z // D_SSD)),
                  pl.BlockSpec((q, LANE), lambda i, j: (i * nc + j, C_DT // LANE)),
                  const((CONV_W, CONV_CH)), const((1, CONV_CH)), const((1, LANE)), const((1, LANE)),
                  const((1, D_SSD)), const((1, D_SSD)), const((LANE, D_SSD)),
                  const((LANE, N_SSD_HEADS * LANE))],
        out_specs=[pl.BlockSpec((q, D_SSD), lambda i, j: (i * nc + j, 0)),
                   pl.BlockSpec((1, N_SSD_HEADS, SSD_HEAD_DIM, D_STATE), lambda i, j: (i, 0, 0, 0)),
                   pl.BlockSpec((1, CONV_W - 1, CONV_CH), lambda i, j: (i, 0, 0))],
        out_shape=[jax.ShapeDtypeStruct((b * t, D_SSD), BF16),
                   jax.ShapeDtypeStruct((b, N_SSD_HEADS, SSD_HEAD_DIM, D_STATE), F32),
                   jax.ShapeDtypeStruct((b, CONV_W - 1, CONV_CH), F32)],
        scratch_shapes=[pltpu.VMEM((q + SUBLANE, CONV_CH), F32), pltpu.VMEM((D_STATE, D_SSD), F32)],
        compiler_params=_cparams(("parallel", "arbitrary"), 48),
        name="ssd_prompt",
    )(proj, proj, proj, cw, cb, dtb, a, dsk, nw, e64, e128)


def _ssd_sample_kernel(xbc_ref, z_ref, dt_ref, cbuf_ref, st_in_ref, cw_ref, cb_ref, dtb_ref, a_ref,
                       dsk_ref, nw_ref, e64_ref, out_ref, st_out_ref, conv_ref,
                       xdt_t, dec_t, b_scr, c_scr, xs_scr, y_scr):
    s = pl.program_id(0)
    n = pl.num_programs(0)
    rows = N_SSD_HEADS * SSD_HEAD_DIM

    @pl.when(s == 0)
    def _():
        x_new = xbc_ref[...]
        acc = cb_ref[...] + cw_ref[CONV_W - 1:CONV_W, :] * x_new
        for k in range(CONV_W - 1):
            acc = acc + cw_ref[k:k + 1, :] * cbuf_ref[k]
        xbc = _silu(acc)
        for k in range(CONV_W - 2):
            conv_ref[k] = cbuf_ref[k + 1]
        conv_ref[CONV_W - 2] = x_new
        xs = xbc[:, :D_SSD]
        lane = lax.broadcasted_iota(jnp.int32, dt_ref.shape, 1)
        dt = jnp.where(lane < N_SSD_HEADS, _softplus(dt_ref[...] + dtb_ref[...]), 0.0)
        e64 = e64_ref[...]
        dt_e = _expand(dt, e64)
        da_e = _expand(dt * a_ref[...], e64)
        xs_scr[...] = xs
        xdt_t[...] = (xs * dt_e).T
        dec_t[...] = jnp.exp(da_e).T
        b_scr[...] = xbc[:, D_SSD:D_SSD + N_SSD_GROUPS * D_STATE]
        c_scr[...] = xbc[:, D_SSD + N_SSD_GROUPS * D_STATE:]

    st = st_in_ref[0].reshape(rows, D_STATE)
    lane = lax.broadcasted_iota(jnp.int32, (rows, xdt_t.shape[1]), 1)
    pick = lane == s
    xcol = jnp.sum(jnp.where(pick, xdt_t[...], 0.0), axis=1, keepdims=True)
    dcol = jnp.sum(jnp.where(pick, dec_t[...], 0.0), axis=1, keepdims=True)
    brow = b_scr[pl.ds(s, 1), :]
    crow = c_scr[pl.ds(s, 1), :]
    bfull = jnp.concatenate(
        [jnp.broadcast_to(brow[:, g * D_STATE:(g + 1) * D_STATE], (GROUP_W, D_STATE))
         for g in range(N_SSD_GROUPS)], axis=0)
    st_new = st * dcol + xcol * bfull
    st_out_ref[0] = st_new.reshape(N_SSD_HEADS, SSD_HEAD_DIM, D_STATE)
    st_b = st_new.astype(BF16)
    ys = []
    for g in range(N_SSD_GROUPS):
        cg = jnp.broadcast_to(crow[:, g * D_STATE:(g + 1) * D_STATE], (2 * SUBLANE, D_STATE)).astype(BF16)
        yg = _dot_nt(cg, st_b[g * GROUP_W:(g + 1) * GROUP_W, :])
        ys.append(yg[0:1, :])
    y_scr[pl.ds(s, 1), :] = jnp.concatenate(ys, axis=1)

    @pl.when(s == n - 1)
    def _():
        out_ref[...] = _gated_norm(y_scr[...], xs_scr[...], z_ref[...], dsk_ref[...],
                                   nw_ref[...]).astype(BF16)


def _ssd_sample(proj, conv_buf_t, state, cw, cb, dtb, a, dsk, nw, e64):
    nb = proj.shape[0]
    const2 = lambda shape: pl.BlockSpec(shape, lambda i: (0, 0))
    return pl.pallas_call(
        _ssd_sample_kernel,
        grid=(nb,),
        in_specs=[pl.BlockSpec((nb, CONV_CH), lambda i: (0, C_XBC // CONV_CH)),
                  pl.BlockSpec((nb, D_SSD), lambda i: (0, C_Z // D_SSD)),
                  pl.BlockSpec((nb, LANE), lambda i: (0, C_DT // LANE)),
                  pl.BlockSpec((CONV_W - 1, nb, CONV_CH), lambda i: (0, 0, 0)),
                  pl.BlockSpec((1, N_SSD_HEADS, SSD_HEAD_DIM, D_STATE), lambda i: (i, 0, 0, 0)),
                  const2((CONV_W, CONV_CH)), const2((1, CONV_CH)), const2((1, LANE)), const2((1, LANE)),
                  const2((1, D_SSD)), const2((1, D_SSD)), const2((LANE, D_SSD))],
        out_specs=[pl.BlockSpec((nb, D_SSD), lambda i: (0, 0)),
                   pl.BlockSpec((1, N_SSD_HEADS, SSD_HEAD_DIM, D_STATE), lambda i: (i, 0, 0, 0)),
                   pl.BlockSpec((CONV_W - 1, nb, CONV_CH), lambda i: (0, 0, 0))],
        out_shape=[jax.ShapeDtypeStruct((nb, D_SSD), BF16),
                   jax.ShapeDtypeStruct(state.shape, F32),
                   jax.ShapeDtypeStruct((CONV_W - 1, nb, CONV_CH), F32)],
        scratch_shapes=[pltpu.VMEM((D_SSD, nb), F32), pltpu.VMEM((D_SSD, nb), F32),
                        pltpu.VMEM((nb, N_SSD_GROUPS * D_STATE), F32),
                        pltpu.VMEM((nb, N_SSD_GROUPS * D_STATE), F32),
                        pltpu.VMEM((nb, D_SSD), F32), pltpu.VMEM((nb, D_SSD), F32)],
        compiler_params=_cparams(("arbitrary",), 48),
        name="ssd_sample",
    )(proj, proj, proj, conv_buf_t, state, cw, cb, dtb, a, dsk, nw, e64)


def _rms(x, w):
    return x * lax.rsqrt(jnp.mean(x * x, axis=1, keepdims=True) + NORM_EPS) * w


def _rope_fold(t):
    lane = lax.broadcasted_iota(jnp.int32, t.shape, 1)
    return jnp.where(lane < QK_ROPE_DIM, t + pltpu.roll(t, QK_ROPE_DIM, 1), 0.0)


def _mla_prep_kernel(cq_ref, ckv_ref, kr_ref, tab_ref, qnw_ref, kvnw_ref, wq_ref, wkv_ref,
                     q_ref, ckvn_ref, krope_ref, *kv_refs, with_kv):
    tab = tab_ref[...]
    qn = _rms(cq_ref[...], qnw_ref[...]).astype(BF16)
    qf = jnp.dot(qn, wq_ref[...], preferred_element_type=F32)
    for h in range(N_MLA_HEADS):
        base = h * QK_PAD
        q_ref[:, base:base + QK_NOPE_DIM] = qf[:, base:base + QK_NOPE_DIM].astype(BF16)
        q_ref[:, base + QK_NOPE_DIM:base + QK_PAD] = _rope_fold(
            qf[:, base + QK_NOPE_DIM:base + QK_PAD] * tab).astype(BF16)
    ckvn = _rms(ckv_ref[...], kvnw_ref[...])
    ckvn_ref[...] = ckvn
    kro = _rope_fold(kr_ref[...] * tab)
    krope_ref[...] = kro[:, :QK_ROPE_DIM]
    if with_kv:
        k_ref, v_ref = kv_refs
        kv = jnp.dot(ckvn.astype(BF16), wkv_ref[...], preferred_element_type=F32)
        kro_b = kro.astype(BF16)
        for h in range(N_MLA_HEADS):
            k_ref[:, h * QK_PAD:h * QK_PAD + QK_NOPE_DIM] = kv[:, h * LANE:(h + 1) * LANE].astype(BF16)
            k_ref[:, h * QK_PAD + QK_NOPE_DIM:(h + 1) * QK_PAD] = kro_b
        v_ref[...] = kv[:, N_MLA_HEADS * LANE:].astype(BF16)


def _mla_prep(proj, tab, qnw, kvnw, wq, wkv, tm, with_kv):
    m = proj.shape[0]
    npos = tab.shape[0] // tm
    const = lambda shape: pl.BlockSpec(shape, lambda i: (0, 0))
    out_specs = [pl.BlockSpec((tm, N_MLA_HEADS * QK_PAD), lambda i: (i, 0)),
                 pl.BlockSpec((tm, KV_LORA), lambda i: (i, 0)),
                 pl.BlockSpec((tm, QK_ROPE_DIM), lambda i: (i, 0))]
    out_shape = [jax.ShapeDtypeStruct((m, N_MLA_HEADS * QK_PAD), BF16),
                 jax.ShapeDtypeStruct((m, KV_LORA), F32),
                 jax.ShapeDtypeStruct((m, QK_ROPE_DIM), F32)]
    if with_kv:
        out_specs += [pl.BlockSpec((tm, N_MLA_HEADS * QK_PAD), lambda i: (i, 0)),
                      pl.BlockSpec((tm, N_MLA_HEADS * V_HEAD_DIM), lambda i: (i, 0))]
        out_shape += [jax.ShapeDtypeStruct((m, N_MLA_HEADS * QK_PAD), BF16),
                      jax.ShapeDtypeStruct((m, N_MLA_HEADS * V_HEAD_DIM), BF16)]
    return pl.pallas_call(
        functools.partial(_mla_prep_kernel, with_kv=with_kv),
        grid=(m // tm,),
        in_specs=[pl.BlockSpec((tm, Q_LORA), lambda i: (i, C_CQ // Q_LORA)),
                  pl.BlockSpec((tm, KV_LORA), lambda i: (i, C_CKV // KV_LORA)),
                  pl.BlockSpec((tm, LANE), lambda i: (i, C_KR // LANE)),
                  pl.BlockSpec((tm, LANE), lambda i: (i % npos, 0)),
                  const((1, Q_LORA)), const((1, KV_LORA)),
                  const((Q_LORA, N_MLA_HEADS * QK_PAD)),
                  const((KV_LORA, 2 * N_MLA_HEADS * V_HEAD_DIM))],
        out_specs=out_specs,
        out_shape=out_shape,
        compiler_params=_cparams(("parallel",), 56),
        name="mla_prep",
    )(proj, proj, proj, tab, qnw, kvnw, wq, wkv)


def _flash_kernel(q_ref, k_ref, v_ref, o_ref, *, tq):
    qi = pl.program_id(2)
    heads = range(FLASH_HEADS)
    qs = [q_ref[:, h * QK_PAD:(h + 1) * QK_PAD] for h in heads]

    def block(j, carry, diagonal):
        off = pl.multiple_of(j * tq, tq)
        out = []
        for h in heads:
            m, l, acc = carry[h]
            k = k_ref[pl.ds(off, tq), h * QK_PAD:(h + 1) * QK_PAD]
            v = v_ref[pl.ds(off, tq), h * V_HEAD_DIM:(h + 1) * V_HEAD_DIM]
            s = _dot_nt(qs[h], k) * (ATTN_SCALE * LOG2_E)
            if diagonal:
                row = lax.broadcasted_iota(jnp.int32, (tq, tq), 0)
                col = lax.broadcasted_iota(jnp.int32, (tq, tq), 1)
                s = jnp.where(col <= row, s, NEG_BIG)
            m_new = jnp.maximum(m, jnp.max(s, axis=1, keepdims=True))
            alpha = jnp.exp2(m - m_new)
            p = jnp.exp2(s - m_new)
            l = alpha * l + jnp.sum(p, axis=1, keepdims=True)
            acc = alpha * acc + jnp.dot(p.astype(BF16), v, preferred_element_type=F32)
            out.append((m_new, l, acc))
        return tuple(out)

    init = tuple((jnp.full((tq, 1), NEG_BIG, F32), jnp.zeros((tq, 1), F32),
                  jnp.zeros((tq, V_HEAD_DIM), F32)) for _ in heads)
    carry = lax.fori_loop(0, qi, functools.partial(block, diagonal=False), init)
    carry = block(qi, carry, True)
    for h in heads:
        _, l, acc = carry[h]
        o_ref[:, h * V_HEAD_DIM:(h + 1) * V_HEAD_DIM] = (acc / l).astype(BF16)


def _flash(q, k, v, b, t, tq):
    nq = t // tq
    return pl.pallas_call(
        functools.partial(_flash_kernel, tq=tq),
        grid=(b, N_MLA_HEADS // FLASH_HEADS, nq),
        in_specs=[pl.BlockSpec((tq, FLASH_HEADS * QK_PAD), lambda i, h, j: (i * nq + j, h)),
                  pl.BlockSpec((t, FLASH_HEADS * QK_PAD), lambda i, h, j: (i, h)),
                  pl.BlockSpec((t, FLASH_HEADS * V_HEAD_DIM), lambda i, h, j: (i, h))],
        out_specs=pl.BlockSpec((tq, FLASH_HEADS * V_HEAD_DIM), lambda i, h, j: (i * nq + j, h)),
        out_shape=jax.ShapeDtypeStruct((b * t, N_MLA_HEADS * V_HEAD_DIM), BF16),
        compiler_params=_cparams(("parallel", "parallel", "arbitrary"), 48),
        name="flash_prompt",
    )(q, k, v)


def _qlat_kernel(qn_ref, qr_ref, w_ref, o_ref):
    o_ref[:, :KV_LORA] = jnp.dot(qn_ref[...], w_ref[0], preferred_element_type=F32).astype(BF16)
    o_ref[:, KV_LORA:] = qr_ref[...]


def _q_latent(q, w_uk_t):
    nb = q.shape[0]
    return pl.pallas_call(
        _qlat_kernel,
        grid=(N_MLA_HEADS,),
        in_specs=[pl.BlockSpec((nb, QK_NOPE_DIM), lambda h: (0, 2 * h)),
                  pl.BlockSpec((nb, LANE), lambda h: (0, 2 * h + 1)),
                  pl.BlockSpec((1, QK_NOPE_DIM, KV_LORA), lambda h: (h, 0, 0))],
        out_specs=pl.BlockSpec((nb, QLAT_W), lambda h: (0, h)),
        out_shape=jax.ShapeDtypeStruct((nb, N_MLA_HEADS * QLAT_W), BF16),
        compiler_params=_cparams(("parallel",), 32),
        name="q_latent",
    )(q, q, w_uk_t)


def _vup_kernel(o_ref, w_ref, out_ref):
    out_ref[...] = jnp.dot(o_ref[...], w_ref[0], preferred_element_type=F32).astype(BF16)


def _v_up(o_lat, w_uv_h):
    nb = o_lat.shape[0]
    return pl.pallas_call(
        _vup_kernel,
        grid=(N_MLA_HEADS,),
        in_specs=[pl.BlockSpec((nb, KV_LORA), lambda h: (0, h)),
                  pl.BlockSpec((1, KV_LORA, V_HEAD_DIM), lambda h: (h, 0, 0))],
        out_specs=pl.BlockSpec((nb, V_HEAD_DIM), lambda h: (0, h)),
        out_shape=jax.ShapeDtypeStruct((nb, N_MLA_HEADS * V_HEAD_DIM), BF16),
        compiler_params=_cparams(("parallel",), 32),
        name="v_up",
    )(o_lat, w_uv_h)


def _page_copies(pt_ref, ckv_hbm, kr_hbm, cbuf, rbuf, sems, first_page, slot, pages):
    copies = []
    for j in range(pages):
        pg = pt_ref[first_page + j]
        copies.append(pltpu.make_async_copy(ckv_hbm.at[0, pg], cbuf.at[slot, j], sems.at[0, slot]))
        copies.append(pltpu.make_async_copy(kr_hbm.at[0, pg], rbuf.at[slot, j], sems.at[1, slot]))
    return copies


def _decode_kernel(pt_ref, q_ref, cnew_ref, rnew_ref, ckv_hbm, kr_hbm, o_ref,
                   cbuf, rbuf, sems, m_scr, l_scr, acc_scr, *, pages):
    s = pl.program_id(1)
    ns = pl.num_programs(1)
    step = pl.program_id(0) * ns + s
    last_step = pl.num_programs(0) * ns - 1
    fetch = functools.partial(_page_copies, pt_ref, ckv_hbm, kr_hbm, cbuf, rbuf, sems, pages=pages)
    q = q_ref[0]
    ql = q[:, :KV_LORA]
    qr = q[:, KV_LORA:KV_LORA + QK_ROPE_DIM]
    group = math.gcd(pages, DECODE_GROUP)

    @pl.when(step == 0)
    def _():
        for cp in fetch(0, 0):
            cp.start()

    @pl.when(s == 0)
    def _():
        m_scr[...] = jnp.full_like(m_scr, NEG_BIG)
        l_scr[...] = jnp.zeros_like(l_scr)
        acc_scr[...] = jnp.zeros_like(acc_scr)

    def attend(slot):
        for cp in fetch(step * pages, slot):
            cp.wait()
        for cp in fetch(jnp.minimum(step + 1, last_step) * pages, 1 - slot):
            cp.start()
        scores, latents = [], []
        for g in range(pages // group):
            cb = cbuf[slot, g * group:(g + 1) * group].reshape(group * PAGE_SIZE, KV_LORA).astype(BF16)
            rb = jnp.concatenate([rbuf[slot, j].astype(BF16) for j in range(g * group, (g + 1) * group)],
                                 axis=1)
            scores.append((_dot_nt(ql, cb) + jnp.dot(qr, rb, preferred_element_type=F32)) * ATTN_SCALE)
            latents.append(cb)
        sc = jnp.concatenate(scores, axis=1)
        m_old = m_scr[...]
        m_new = jnp.maximum(m_old, jnp.max(sc, axis=1, keepdims=True))
        alpha = jnp.exp(m_old - m_new)
        p = jnp.exp(sc - m_new)
        l_scr[...] = alpha * l_scr[...] + jnp.sum(p, axis=1, keepdims=True)
        pb = p.astype(BF16)
        keys = group * PAGE_SIZE
        pv = jnp.dot(pb[:, :keys], latents[0], preferred_element_type=F32)
        for g in range(1, pages // group):
            pv = pv + jnp.dot(pb[:, g * keys:(g + 1) * keys], latents[g], preferred_element_type=F32)
        acc_scr[...] = alpha * acc_scr[...] + pv
        m_scr[...] = m_new

    for slot in range(2):
        @pl.when(lax.rem(step, 2) == slot)
        def _(slot=slot):
            attend(slot)

    @pl.when(step == last_step)
    def _():
        for cp in fetch(last_step * pages, 1 - lax.rem(step, 2)):
            cp.wait()

    @pl.when(s == pl.num_programs(1) - 1)
    def _():
        cn = cnew_ref[0].astype(BF16).astype(F32)
        rn = rnew_ref[0].astype(BF16).astype(F32)
        s_new = (jnp.sum(ql.astype(F32) * cn, axis=1, keepdims=True)
                 + jnp.sum(qr.astype(F32) * rn, axis=1, keepdims=True)) * ATTN_SCALE
        m_old = m_scr[...]
        m_fin = jnp.maximum(m_old, s_new)
        alpha = jnp.exp(m_old - m_fin)
        p_new = jnp.exp(s_new - m_fin)
        l_fin = alpha * l_scr[...] + p_new
        acc = alpha * acc_scr[...] + p_new.astype(BF16).astype(F32) * cn
        o_ref[0] = (acc / l_fin).astype(BF16)


def _decode(q_cat, ckv_new, kr_new, cache_ckv, cache_krope_t, page_table, pages):
    nb, n_pages = page_table.shape
    steps = n_pages // pages

    grid_spec = pltpu.PrefetchScalarGridSpec(
        num_scalar_prefetch=1,
        grid=(nb, steps),
        in_specs=[pl.BlockSpec((1, N_MLA_HEADS, QLAT_W), lambda b, s, pt: (b, 0, 0)),
                  pl.BlockSpec((1, 1, KV_LORA), lambda b, s, pt: (b, 0, 0)),
                  pl.BlockSpec((1, 1, QK_ROPE_DIM), lambda b, s, pt: (b, 0, 0)),
                  pl.BlockSpec(memory_space=pl.ANY), pl.BlockSpec(memory_space=pl.ANY)],
        out_specs=pl.BlockSpec((1, N_MLA_HEADS, KV_LORA), lambda b, s, pt: (b, 0, 0)),
        scratch_shapes=[pltpu.VMEM((2, pages, PAGE_SIZE, KV_LORA), F32),
                        pltpu.VMEM((2, pages, QK_ROPE_DIM, PAGE_SIZE), F32),
                        pltpu.SemaphoreType.DMA((2, 2)),
                        pltpu.VMEM((N_MLA_HEADS, 1), F32), pltpu.VMEM((N_MLA_HEADS, 1), F32),
                        pltpu.VMEM((N_MLA_HEADS, KV_LORA), F32)],
    )
    return pl.pallas_call(
        functools.partial(_decode_kernel, pages=pages),
        grid_spec=grid_spec,
        out_shape=jax.ShapeDtypeStruct((nb, N_MLA_HEADS, KV_LORA), BF16),
        compiler_params=_cparams(("arbitrary", "arbitrary"), 48),
        name="decode_attn",
    )(page_table.reshape(-1), q_cat, ckv_new, kr_new, cache_ckv, cache_krope_t)


def _layer_norm(x, g, b):
    mu = jnp.mean(x, axis=1, keepdims=True)
    xc = x - mu
    var = jnp.mean(xc * xc, axis=1, keepdims=True)
    return xc * lax.rsqrt(var + NORM_EPS) * g + b


def _outproj_kernel(*refs, tn, aliased):
    if aliased:
        refs = refs[1:]
    ssd_ref, mla_ref, wt_ref, wb_ref, x_ref, g_ref, b_ref, o_ref = refs
    j = pl.program_id(1)
    mix = (jnp.dot(ssd_ref[...], wt_ref[...], preferred_element_type=F32)
           + jnp.dot(mla_ref[...], wb_ref[...], preferred_element_type=F32))
    val = DEEPNORM_ALPHA * x_ref[...] + mix
    nj = D_MODEL // tn
    for jj in range(nj):
        @pl.when(j == jj)
        def _(jj=jj):
            o_ref[:, jj * tn:(jj + 1) * tn] = val

    @pl.when(j == nj - 1)
    def _():
        o_ref[...] = _layer_norm(o_ref[...], g_ref[...], b_ref[...])


def _outproj_ln(ssd, mla, w_out_b, x, g, b, tm, total_rows, row_block0, buf=None):
    m = x.shape[0]
    tn = 512
    aliased = buf is not None
    in_specs = [pl.BlockSpec((tm, D_SSD), lambda i, j: (i, 0)),
                pl.BlockSpec((tm, D_SSD), lambda i, j: (i, 0)),
                pl.BlockSpec((D_SSD, tn), lambda i, j: (0, j)),
                pl.BlockSpec((D_SSD, tn), lambda i, j: (1, j)),
                pl.BlockSpec((tm, tn), lambda i, j: (i, j)),
                pl.BlockSpec((1, D_MODEL), lambda i, j: (0, 0)),
                pl.BlockSpec((1, D_MODEL), lambda i, j: (0, 0))]
    args = [ssd, mla, w_out_b, w_out_b, x, g, b]
    if aliased:
        in_specs = [pl.BlockSpec(memory_space=pl.ANY)] + in_specs
        args = [buf] + args
    return pl.pallas_call(
        functools.partial(_outproj_kernel, tn=tn, aliased=aliased),
        grid=(m // tm, D_MODEL // tn),
        in_specs=in_specs,
        out_specs=pl.BlockSpec((tm, D_MODEL), lambda i, j: (i + row_block0, 0)),
        out_shape=jax.ShapeDtypeStruct((total_rows, D_MODEL), F32),
        input_output_aliases={0: 0} if aliased else {},
        compiler_params=_cparams(("parallel", "arbitrary"), 56),
        name="out_proj_ln",
    )(*args)


def _router_kernel(h_ref, wh_ref, wm_ref, bias_ref, ids_ref, gates_ref):
    hh, hm, _ = _split3(h_ref[...])
    wh = wh_ref[...]
    d = functools.partial(jnp.dot, preferred_element_type=F32)
    lg = d(hh, wh) + (d(hh, wm_ref[...]) + d(hm, wh)) + bias_ref[...]
    lane = lax.broadcasted_iota(jnp.int32, lg.shape, 1)
    big = jnp.int32(1 << 20)
    gmask = lane < N_EXPERT_GROUPS
    gl = jnp.where(gmask, lg, -jnp.inf)
    ge = jnp.exp(gl - jnp.max(gl, axis=1, keepdims=True))
    gp = ge / jnp.sum(ge, axis=1, keepdims=True)
    g_val = jnp.max(gp, axis=1, keepdims=True)
    g_idx = jnp.min(jnp.where(gmask & (gp == g_val), lane, big), axis=1, keepdims=True)
    eid = lane - N_EXPERT_GROUPS
    emask = (eid >= 0) & (eid < N_EXPERTS) & (lax.shift_right_arithmetic(eid, 3) == g_idx)
    el = jnp.where(emask, lg, -jnp.inf)
    ee = jnp.exp(el - jnp.max(el, axis=1, keepdims=True))
    ep = jnp.where(emask, ee / jnp.sum(ee, axis=1, keepdims=True), -1.0)
    v1 = jnp.max(ep, axis=1, keepdims=True)
    i1 = jnp.min(jnp.where(emask & (ep == v1), lane, big), axis=1, keepdims=True)
    ep2 = jnp.where(lane == i1, -1.0, ep)
    v2 = jnp.max(ep2, axis=1, keepdims=True)
    i2 = jnp.min(jnp.where(emask & (lane != i1) & (ep2 == v2), lane, big), axis=1, keepdims=True)
    tot = v1 + v2
    ids_ref[...] = jnp.where(lane == 0, i1 - N_EXPERT_GROUPS,
                             jnp.where(lane == 1, i2 - N_EXPERT_GROUPS, 0))
    gates_ref[...] = jnp.where(lane == 0, g_val * (v1 / tot), jnp.where(lane == 1, g_val * (v2 / tot), 0.0))


def _router(h, wh, wm, bias, tm):
    t = h.shape[0]
    const = lambda shape: pl.BlockSpec(shape, lambda i: (0, 0))
    return pl.pallas_call(
        _router_kernel,
        grid=(t // tm,),
        in_specs=[pl.BlockSpec((tm, D_MODEL), lambda i: (i, 0)),
                  const((D_MODEL, LANE)), const((D_MODEL, LANE)), const((1, LANE))],
        out_specs=[pl.BlockSpec((tm, LANE), lambda i: (i, 0)), pl.BlockSpec((tm, LANE), lambda i: (i, 0))],
        out_shape=[jax.ShapeDtypeStruct((t, LANE), jnp.int32), jax.ShapeDtypeStruct((t, LANE), F32)],
        compiler_params=_cparams(("parallel",), 48),
        name="router",
    )(h, wh, wm, bias)


def _row_copy(src, src_row, dst, dst_row, sem):
    return pltpu.make_async_copy(src.at[pl.ds(src_row, 1)], dst.at[pl.ds(dst_row, 1)], sem)


def _for_rows(count, fn):
    full = count // ROW_UNROLL

    def body_unrolled(c, carry):
        for u in range(ROW_UNROLL):
            fn(c * ROW_UNROLL + u)
        return carry

    def body_one(r, carry):
        fn(r)
        return carry

    lax.fori_loop(0, full, body_unrolled, 0)
    lax.fori_loop(full * ROW_UNROLL, count, body_one, 0)


def _moe_kernel(blk_e, blk_start, blk_n, n_live, tok, dst, h_hbm, wg_ref, wu_ref, wd_ref, y_hbm,
                xg, xb, acc, sem_in, sem_out):
    i = pl.program_id(0)
    n = pl.program_id(1)
    last_i = pl.num_programs(0) - 1
    last_n = pl.num_programs(1) - 1
    nrows = blk_n[i]

    def start_gather(blk):
        base = blk_start[blk]
        _for_rows(blk_n[blk], lambda r: _row_copy(h_hbm, tok[base + r], xg, r, sem_in).start())

    def start_scatter(blk):
        base = blk_start[blk]
        _for_rows(blk_n[blk], lambda r: _row_copy(acc, r, y_hbm, dst[base + r], sem_out).start())

    def wait_rows(src, dst_ref, sem, count):
        _for_rows(count, lambda r: _row_copy(src, 0, dst_ref, 0, sem).wait())

    @pl.when((i == 0) & (n == 0))
    def _():
        xg[...] = jnp.zeros_like(xg)
        acc[...] = jnp.zeros_like(acc)
        start_gather(0)

    @pl.when(n == 0)
    def _():
        wait_rows(h_hbm, xg, sem_in, nrows)

        @pl.when(i > 0)
        def _():
            wait_rows(acc, y_hbm, sem_out, blk_n[jnp.maximum(i - 1, 0)])

    for k in range(1, MOE_BLK // MOE_SUB + 1):
        rows = k * MOE_SUB

        @pl.when((nrows > rows - MOE_SUB) & (nrows <= rows))
        def _(rows=rows):
            @pl.when(n == 0)
            def _():
                xb[0:rows, :] = xg[0:rows, :].astype(BF16)

            x = xb[0:rows, :]
            hid = (_silu(jnp.dot(x, wg_ref[...].astype(BF16), preferred_element_type=F32))
                   * jnp.dot(x, wu_ref[...].astype(BF16), preferred_element_type=F32)).astype(BF16)
            wd = wd_ref[...].astype(BF16)
            for c in range(D_MODEL // MOE_DOWN_TN):
                cs = slice(c * MOE_DOWN_TN, (c + 1) * MOE_DOWN_TN)
                part = jnp.dot(hid, wd[:, cs], preferred_element_type=F32)
                acc[0:rows, cs] = jnp.where(n > 0, acc[0:rows, cs], 0.0) + part

    @pl.when((n == 0) & (i < last_i))
    def _():
        start_gather(i + 1)

    @pl.when(n == last_n)
    def _():
        start_scatter(i)

        @pl.when(i == last_i)
        def _():
            wait_rows(acc, y_hbm, sem_out, nrows)


def _moe(h_all, blk_e, blk_start, blk_n, n_live, tok, dst, w_gate_e, w_up_e, w_down_e, nb_max):
    t = h_all.shape[0]
    nt = D_EXPERT // MOE_TN

    def live_n(i, n, nl):
        return jnp.where(i < nl[0], n, nt - 1)

    grid_spec = pltpu.PrefetchScalarGridSpec(
        num_scalar_prefetch=6,
        grid=(nb_max, nt),
        in_specs=[pl.BlockSpec(memory_space=pl.ANY),
                  pl.BlockSpec((None, None, D_MODEL, MOE_TN),
                               lambda i, n, be, bs, bn, nl, tk, ds: (0, be[i], 0, live_n(i, n, nl))),
                  pl.BlockSpec((None, None, D_MODEL, MOE_TN),
                               lambda i, n, be, bs, bn, nl, tk, ds: (0, be[i], 0, live_n(i, n, nl))),
                  pl.BlockSpec((None, None, MOE_TN, D_MODEL),
                               lambda i, n, be, bs, bn, nl, tk, ds: (0, be[i], live_n(i, n, nl), 0))],
        out_specs=pl.BlockSpec(memory_space=pl.ANY),
        scratch_shapes=[pltpu.VMEM((MOE_BLK, D_MODEL), F32), pltpu.VMEM((MOE_BLK, D_MODEL), BF16),
                        pltpu.VMEM((MOE_BLK, D_MODEL), F32),
                        pltpu.SemaphoreType.DMA(()), pltpu.SemaphoreType.DMA(())],
    )
    return pl.pallas_call(
        _moe_kernel,
        grid_spec=grid_spec,
        out_shape=jax.ShapeDtypeStruct((TOP_K * t, D_MODEL), F32),
        compiler_params=_cparams(("arbitrary", "arbitrary"), 56),
        name="moe_experts",
    )(blk_e, blk_start, blk_n, n_live, tok, dst, h_all, w_gate_e, w_up_e, w_down_e)


def _dispatch_plan(ids, nb_max):
    t = ids.shape[0]
    n_assign = TOP_K * t
    e_flat = ids.reshape(-1)
    order = jnp.argsort(e_flat).astype(jnp.int32)
    counts = jnp.bincount(e_flat, length=N_EXPERTS).astype(jnp.int32)
    start = jnp.cumsum(counts) - counts
    nblk = (counts + MOE_BLK - 1) // MOE_BLK
    blk_end = jnp.cumsum(nblk)
    n_live = blk_end[-1]
    bi = jnp.arange(nb_max, dtype=jnp.int32)
    live = bi < n_live
    be = jnp.minimum(jnp.searchsorted(blk_end, bi, side='right'), N_EXPERTS - 1).astype(jnp.int32)
    last_e = jnp.max(jnp.where(counts > 0, jnp.arange(N_EXPERTS, dtype=jnp.int32), 0))
    be = jnp.where(live, be, last_e)
    k = bi - (blk_end[be] - nblk[be])
    bstart = jnp.where(live, start[be] + k * MOE_BLK, 0).astype(jnp.int32)
    bn = jnp.where(live, jnp.clip(counts[be] - k * MOE_BLK, 0, MOE_BLK), 0).astype(jnp.int32)
    tok = order // TOP_K
    dst = (order % TOP_K) * t + tok
    return be, bstart, bn, n_live.reshape(1).astype(jnp.int32), tok, dst


def _ln2_kernel(h_ref, y0_ref, y1_ref, gt_ref, g_ref, b_ref, op_ref, os_ref, *, n_prompt_blocks):
    i = pl.program_id(0)
    gt = gt_ref[...]
    ffn = gt[:, 0:1] * y0_ref[...] + gt[:, 1:2] * y1_ref[...]
    out = _layer_norm(DEEPNORM_ALPHA * h_ref[...] + ffn, g_ref[...], b_ref[...])

    @pl.when(i < n_prompt_blocks)
    def _():
        op_ref[...] = out

    @pl.when(i >= n_prompt_blocks)
    def _():
        os_ref[...] = out


def _ln2(h_all, y2, gates, g, b, n_prompt, tm):
    t = h_all.shape[0]
    nblk = t // tm
    npb = n_prompt // tm
    const = lambda shape: pl.BlockSpec(shape, lambda i: (0, 0))
    return pl.pallas_call(
        functools.partial(_ln2_kernel, n_prompt_blocks=npb),
        grid=(nblk,),
        in_specs=[pl.BlockSpec((tm, D_MODEL), lambda i: (i, 0)),
                  pl.BlockSpec((tm, D_MODEL), lambda i: (i, 0)),
                  pl.BlockSpec((tm, D_MODEL), lambda i: (i + nblk, 0)),
                  pl.BlockSpec((tm, LANE), lambda i: (i, 0)),
                  const((1, D_MODEL)), const((1, D_MODEL))],
        out_specs=[pl.BlockSpec((tm, D_MODEL), lambda i: (jnp.minimum(i, npb - 1), 0)),
                   pl.BlockSpec((tm, D_MODEL), lambda i: (jnp.maximum(i - npb, 0), 0))],
        out_shape=[jax.ShapeDtypeStruct((n_prompt, D_MODEL), F32),
                   jax.ShapeDtypeStruct((t - n_prompt, D_MODEL), F32)],
        compiler_params=_cparams(("arbitrary",), 48),
        name="ffn_ln",
    )(h_all, y2, y2, gates, g, b)


def _rot_cols(w):
    half = w.shape[-1] // 2
    return jnp.concatenate([-w[..., half:], w[..., :half]], axis=-1)


def _rope_table(pos):
    half = QK_ROPE_DIM // 2
    inv_freq = ROPE_THETA ** (-jnp.arange(half, dtype=F32) / half)
    ang = pos.astype(F32)[:, None] * inv_freq
    cos, sin = jnp.cos(ang), jnp.sin(ang)
    return jnp.concatenate([cos, cos, sin, sin], axis=1)


def _expander(width):
    k = lax.broadcasted_iota(jnp.int32, (LANE, N_SSD_HEADS * width), 0)
    c = lax.broadcasted_iota(jnp.int32, (LANE, N_SSD_HEADS * width), 1)
    return (k == c // width).astype(BF16)


def _row_tile(m, pref):
    while m % pref:
        pref //= 2
    return pref


def kernel(x_prompt, x_sample, cache_ckv, cache_krope, state_ssm, state_conv, page_table, w_in, conv_w, conv_b, dt_bias, a_log, d_skip, ssd_norm_w, q_norm_w, w_uq, kv_norm_w, w_uk, w_uv, w_out, ln1_g, ln1_b, w_router_group, b_router_group, w_router_expert, b_router_expert, w_gate_e, w_up_e, w_down_e, ln2_g, ln2_b):
    b, t, _ = x_prompt.shape
    nb, dec_t, _ = x_sample.shape
    assert dec_t == 1 and w_in.shape[0] == 1
    n_pages = page_table.shape[1]
    past = n_pages * PAGE_SIZE
    n_prompt = b * t
    n_tok = n_prompt + nb

    wt = jnp.swapaxes(w_in[0], 0, 1)
    s0 = D_SSD
    s1 = s0 + CONV_CH
    s2 = s1 + N_SSD_HEADS
    s3 = s2 + Q_LORA
    s4 = s3 + KV_LORA
    w_kr_t = wt[s4:]
    half = QK_ROPE_DIM // 2
    w_all_t = jnp.concatenate(
        [wt[s0:s1], wt[s2:s3], wt[:s0], wt[s3:s4], w_kr_t, -w_kr_t[half:], w_kr_t[:half], wt[s1:s2],
         jnp.zeros((LANE - N_SSD_HEADS, D_MODEL), F32)], axis=0).astype(BF16)
    pad_heads = lambda v: jnp.pad(v, (0, LANE - N_SSD_HEADS)).reshape(1, LANE)
    dtb = pad_heads(dt_bias[0])
    a_neg = pad_heads(-jnp.exp(a_log[0].astype(F32)))
    dsk = jnp.repeat(d_skip[0], SSD_HEAD_DIM).reshape(1, D_SSD)
    nw = ssd_norm_w[0].reshape(1, D_SSD)
    cw = conv_w[0]
    cb = conv_b[0].reshape(1, CONV_CH)
    e64 = _expander(SSD_HEAD_DIM)
    e128 = _expander(LANE)
    wq = w_uq[0]
    wq_r = wq[..., QK_NOPE_DIM:]
    wq_all = jnp.concatenate([wq[..., :QK_NOPE_DIM], wq_r, _rot_cols(wq_r)], axis=-1)
    wq_all = wq_all.reshape(Q_LORA, N_MLA_HEADS * QK_PAD).astype(BF16)
    wkv = jnp.concatenate([w_uk[0].reshape(KV_LORA, -1), w_uv[0].reshape(KV_LORA, -1)], axis=1).astype(BF16)
    w_uk_t = jnp.transpose(w_uk[0], (1, 2, 0)).astype(BF16)
    w_uv_h = jnp.transpose(w_uv[0], (1, 0, 2)).astype(BF16)
    w_out_b = w_out[0].astype(BF16)
    qnw = q_norm_w[0].reshape(1, Q_LORA)
    kvnw = kv_norm_w[0].reshape(1, KV_LORA)
    g1, b1 = ln1_g[0].reshape(1, D_MODEL), ln1_b[0].reshape(1, D_MODEL)
    g2, b2 = ln2_g[0].reshape(1, D_MODEL), ln2_b[0].reshape(1, D_MODEL)
    w_r = jnp.concatenate([w_router_group[0], w_router_expert[0],
                           jnp.zeros((D_MODEL, LANE - N_EXPERT_GROUPS - N_EXPERTS), F32)], axis=1)
    w_r_hi = w_r.astype(BF16)
    w_r_mid = (w_r - w_r_hi.astype(F32)).astype(BF16)
    b_r = jnp.concatenate([b_router_group[0], b_router_expert[0],
                           jnp.zeros((LANE - N_EXPERT_GROUPS - N_EXPERTS,), F32)]).reshape(1, LANE)

    xp = x_prompt.reshape(n_prompt, D_MODEL)
    xsm = x_sample.reshape(nb, D_MODEL)

    proj_p = _matmul(xp, w_all_t, _row_tile(n_prompt, 512), 768)
    ssd_p, ssm_p, conv_p = _ssd_prompt(proj_p, b, t, cw, cb, dtb, a_neg, dsk, nw, e64, e128)
    tm_p = _row_tile(t, 256)
    tab_p = _rope_table(jnp.arange(t))
    q_p, ckv_p, kr_p, k_p, v_p = _mla_prep(proj_p, tab_p, qnw, kvnw, wq_all, wkv, tm_p, True)
    mla_p = _flash(q_p, k_p, v_p, b, t, _row_tile(t, 512))
    tm_o = _row_tile(n_prompt, 512)
    h_all = _outproj_ln(ssd_p, mla_p, w_out_b, xp, g1, b1, tm_o, n_tok, 0,
                        buf=jnp.zeros((n_tok, D_MODEL), F32))

    proj_s = _matmul(xsm, w_all_t, nb, 768)
    conv_buf_t = jnp.transpose(state_conv[0], (1, 0, 2))
    ssd_s, ssm_s, conv_s_t = _ssd_sample(proj_s, conv_buf_t, state_ssm[0], cw, cb, dtb, a_neg, dsk, nw, e64)
    tab_s = _rope_table(jnp.full((nb,), past, jnp.int32))
    q_s, ckv_s, kr_s = _mla_prep(proj_s, tab_s, qnw, kvnw, wq_all, wkv, nb, False)
    q_cat = _q_latent(q_s, w_uk_t).reshape(nb, N_MLA_HEADS, QLAT_W)
    pages = math.gcd(n_pages, DECODE_PAGES)
    o_lat = _decode(q_cat, ckv_s.reshape(nb, 1, KV_LORA), kr_s.reshape(nb, 1, QK_ROPE_DIM),
                    cache_ckv, jnp.swapaxes(cache_krope, 2, 3), page_table, pages)
    mla_s = _v_up(o_lat.reshape(nb, N_MLA_HEADS * KV_LORA), w_uv_h)
    h_all = _outproj_ln(ssd_s, mla_s, w_out_b, xsm, g1, b1, nb, n_tok, n_prompt // nb, buf=h_all)

    tm_r = _row_tile(math.gcd(n_prompt, nb), 128)
    ids, gates = _router(h_all, w_r_hi, w_r_mid, b_r, _row_tile(n_tok, 640))
    nb_max = (TOP_K * n_tok) // MOE_BLK + N_EXPERTS
    be, bstart, bn, n_live, tok, dst = _dispatch_plan(ids[:, :TOP_K], nb_max)
    y2 = _moe(h_all, be, bstart, bn, n_live, tok, dst, w_gate_e, w_up_e, w_down_e, nb_max)
    y_p, y_s = _ln2(h_all, y2, gates, g2, b2, n_prompt, tm_r)

    return (y_p.reshape(b, t, D_MODEL), y_s.reshape(nb, 1, D_MODEL),
            ckv_p.reshape(1, b, t, KV_LORA), kr_p.reshape(1, b, t, QK_ROPE_DIM),
            ssm_p[None], conv_p[None],
            ckv_s.reshape(1, nb, 1, KV_LORA), kr_s.reshape(1, nb, 1, QK_ROPE_DIM),
            ssm_s[None], jnp.transpose(conv_s_t, (1, 0, 2))[None])
```

```python
import functools
import math

import jax
import jax.numpy as jnp
from jax import lax
from jax.experimental import pallas as pl
from jax.experimental.pallas import tpu as pltpu

F32 = jnp.float32
BF16 = jnp.bfloat16

D_MODEL = 4096
D_SSD = 2048
SSD_HEAD_DIM = 64
N_SSD_HEADS = 32
N_SSD_GROUPS = 4
GROUP_W = D_SSD // N_SSD_GROUPS
D_STATE = 128
CONV_W = 4
CONV_CH = D_SSD + 2 * N_SSD_GROUPS * D_STATE
SSD_CHUNK = 128
N_MLA_HEADS = 16
V_HEAD_DIM = 128
QK_NOPE_DIM = 128
QK_ROPE_DIM = 64
Q_LORA = 1024
KV_LORA = 512
ROPE_THETA = 10000.0
ATTN_SCALE = (QK_NOPE_DIM + QK_ROPE_DIM) ** -0.5
PAGE_SIZE = 128
N_EXPERT_GROUPS = 8
EXPERTS_PER_GROUP = 8
N_EXPERTS = 64
TOP_K = 2
D_EXPERT = 1024
DEPTH = 1
DEEPNORM_ALPHA = (2 * DEPTH) ** 0.25
NORM_EPS = 1e-5

LANE = 128
SUBLANE = 8
QK_PAD = 256
QLAT_W = 640

C_XBC, C_CQ, C_Z, C_CKV, C_KR, C_DT = 0, 3072, 4096, 6144, 6656, 6784
PROJ_W = 6912

FLASH_HEADS = 2
DECODE_PAGES = 32
DECODE_GROUP = 8
MOE_BLK = 512
MOE_SUB = 64
MOE_TN = 256
MOE_DOWN_TN = 1024
ROW_UNROLL = 8
NEG_BIG = -1e30
LOG2_E = 1.4426950408889634


def _cparams(sem, vmem_mb):
    return pltpu.CompilerParams(dimension_semantics=sem, vmem_limit_bytes=vmem_mb * 1024 * 1024)


def _sigmoid(x):
    return 1.0 / (1.0 + jnp.exp(-x))


def _silu(x):
    return x * _sigmoid(x)


def _split3(x):
    hi = x.astype(BF16)
    r1 = x - hi.astype(F32)
    mid = r1.astype(BF16)
    lo = (r1 - mid.astype(F32)).astype(BF16)
    return hi, mid, lo


def _expand(x, e):
    hi, mid, lo = _split3(x)
    d = functools.partial(jnp.dot, preferred_element_type=F32)
    return d(hi, e) + d(mid, e) + d(lo, e)


def _dot_nt(a, b):
    return lax.dot_general(a, b, (((1,), (1,)), ((), ())), preferred_element_type=F32)


def _dot_tn(a, b):
    return lax.dot_general(a, b, (((0,), (0,)), ((), ())), preferred_element_type=F32)


def _mm_kernel(x_ref, w_ref, o_ref, xb_ref):
    @pl.when(pl.program_id(1) == 0)
    def _():
        xb_ref[...] = x_ref[...].astype(BF16)

    o_ref[...] = _dot_nt(xb_ref[...], w_ref[...])


def _matmul(x, w_t, tm, tn):
    m, k = x.shape
    n = w_t.shape[0]
    return pl.pallas_call(
        _mm_kernel,
        grid=(m // tm, n // tn),
        in_specs=[pl.BlockSpec((tm, k), lambda i, j: (i, 0)),
                  pl.BlockSpec((tn, k), lambda i, j: (j, 0))],
        out_specs=pl.BlockSpec((tm, tn), lambda i, j: (i, j)),
        out_shape=jax.ShapeDtypeStruct((m, n), F32),
        scratch_shapes=[pltpu.VMEM((tm, k), BF16)],
        compiler_params=_cparams(("parallel", "arbitrary"), 48),
        name="in_proj",
    )(x, w_t)


def _softplus(x):
    return jnp.maximum(x, 0.0) + jnp.log1p(jnp.exp(-jnp.abs(x)))


def _gated_norm(y, xs, z, dsk, nw):
    y = (y + dsk * xs) * _silu(z)
    outs = []
    for g in range(N_SSD_GROUPS):
        yg = y[:, g * GROUP_W:(g + 1) * GROUP_W]
        ms = jnp.mean(yg * yg, axis=1, keepdims=True)
        outs.append(yg * lax.rsqrt(ms + NORM_EPS))
    return jnp.concatenate(outs, axis=1) * nw


def _cumsum_rows(x):
    row = lax.broadcasted_iota(jnp.int32, x.shape, 0)
    s = 1
    while s < x.shape[0]:
        x = x + jnp.where(row >= s, pltpu.roll(x, s, 0), 0.0)
        s *= 2
    return x


def _ssd_prompt_kernel(xbc_ref, z_ref, dt_ref, cw_ref, cb_ref, dtb_ref, a_ref, dsk_ref, nw_ref,
                       e64_ref, e128_ref, out_ref, st_ref, conv_ref, xs_scr, st_scr):
    c = pl.program_id(1)
    q = SSD_CHUNK

    @pl.when(c == 0)
    def _():
        xs_scr[0:SUBLANE, :] = jnp.zeros((SUBLANE, CONV_CH), F32)
        st_scr[...] = jnp.zeros_like(st_scr)

    xs_scr[SUBLANE:SUBLANE + q, :] = xbc_ref[...]
    acc = cb_ref[...] + cw_ref[CONV_W - 1:CONV_W, :] * xs_scr[SUBLANE:SUBLANE + q, :]
    for k in range(CONV_W - 1):
        off = SUBLANE - (CONV_W - 1) + k
        acc = acc + cw_ref[k:k + 1, :] * xs_scr[off:off + q, :]
    xbc = _silu(acc)
    conv_ref[0] = xs_scr[q + SUBLANE - (CONV_W - 1):q + SUBLANE, :]
    xs_scr[0:SUBLANE, :] = xs_scr[q:q + SUBLANE, :]

    xs = xbc[:, :D_SSD]
    bm = xbc[:, D_SSD:D_SSD + N_SSD_GROUPS * D_STATE]
    cm = xbc[:, D_SSD + N_SSD_GROUPS * D_STATE:]

    lane = lax.broadcasted_iota(jnp.int32, (q, LANE), 1)
    row = lax.broadcasted_iota(jnp.int32, (q, LANE), 0)
    dt = jnp.where(lane < N_SSD_HEADS, _softplus(dt_ref[...] + dtb_ref[...]), 0.0)
    a_cs = _cumsum_rows(dt * a_ref[...])
    a_cs_t = a_cs.T
    e64 = e64_ref[...]
    dt_e = _expand(dt, e64)
    acs_e = _expand(a_cs, e64)
    acs_b = _expand(a_cs, e128_ref[...])
    xdt = xs * dt_e
    last = acs_e[q - 1:q, :]
    xw = (xdt * jnp.exp(last - acs_e)).astype(BF16)
    xdt_b = xdt.astype(BF16)
    chunk_decay = jnp.exp(last)
    in_decay = jnp.exp(acs_e)
    causal = row >= lane

    ys = []
    for g in range(N_SSD_GROUPS):
        gs = slice(g * GROUP_W, (g + 1) * GROUP_W)
        bg = bm[:, g * D_STATE:(g + 1) * D_STATE].astype(BF16)
        cg = cm[:, g * D_STATE:(g + 1) * D_STATE].astype(BF16)
        cb = _dot_nt(cg, bg)
        st_g = st_scr[:, gs]
        y_off = jnp.dot(cg, st_g.astype(BF16), preferred_element_type=F32) * in_decay[:, gs]
        yd = []
        for pr in range(GROUP_W // LANE):
            h0 = g * (N_SSD_HEADS // N_SSD_GROUPS) + 2 * pr
            ms = []
            for h in (h0, h0 + 1):
                seg = acs_b[:, h * LANE:(h + 1) * LANE] - a_cs_t[h:h + 1, :]
                dec = jnp.exp(jnp.where(causal, seg, -jnp.inf))
                ms.append((cb * dec).astype(BF16))
            lhs = jnp.concatenate(ms, axis=1)
            xp = xdt_b[:, h0 * SSD_HEAD_DIM:(h0 + 2) * SSD_HEAD_DIM]
            zero = jnp.zeros_like(xp)
            rhs = jnp.concatenate([jnp.where(lane < SSD_HEAD_DIM, xp, zero),
                                   jnp.where(lane >= SSD_HEAD_DIM, xp, zero)], axis=0)
            yd.append(jnp.dot(lhs, rhs, preferred_element_type=F32))
        ys.append(jnp.concatenate(yd, axis=1) + y_off)
        st_scr[:, gs] = st_g * chunk_decay[:, gs] + _dot_tn(bg, xw[:, gs])
    y = jnp.concatenate(ys, axis=1)

    out_ref[...] = _gated_norm(y, xs, z_ref[...], dsk_ref[...], nw_ref[...]).astype(BF16)

    @pl.when(c == pl.num_programs(1) - 1)
    def _():
        st_ref[0] = st_scr[...].T.reshape(N_SSD_HEADS, SSD_HEAD_DIM, D_STATE)


def _ssd_prompt(proj, b, t, cw, cb, dtb, a, dsk, nw, e64, e128):
    nc = t // SSD_CHUNK
    q = SSD_CHUNK
    const = lambda shape: pl.BlockSpec(shape, lambda i, j: (0, 0))
    return pl.pallas_call(
        _ssd_prompt_kernel,
        grid=(b, nc),
        in_specs=[pl.BlockSpec((q, CONV_CH), lambda i, j: (i * nc + j, C_XBC // CONV_CH)),
                  pl.BlockSpec((q, D_SSD), lambda i, j: (i * nc + j, C_Z // D_SSD)),
                  pl.BlockSpec((q, LANE), lambda i, j: (i * nc + j, C_DT // LANE)),
                  const((CONV_W, CONV_CH)), const((1, CONV_CH)), const((1, LANE)), const((1, LANE)),
                  const((1, D_SSD)), const((1, D_SSD)), const((LANE, D_SSD)),
                  const((LANE, N_SSD_HEADS * LANE))],
        out_specs=[pl.BlockSpec((q, D_SSD), lambda i, j: (i * nc + j, 0)),
                   pl.BlockSpec((1, N_SSD_HEADS, SSD_HEAD_DIM, D_STATE), lambda i, j: (i, 0, 0, 0)),
                   pl.BlockSpec((1, CONV_W - 1, CONV_CH), lambda i, j: (i, 0, 0))],
        out_shape=[jax.ShapeDtypeStruct((b * t, D_SSD), BF16),
                   jax.ShapeDtypeStruct((b, N_SSD_HEADS, SSD_HEAD_DIM, D_STATE), F32),
                   jax.ShapeDtypeStruct((b, CONV_W - 1, CONV_CH), F32)],
        scratch_shapes=[pltpu.VMEM((q + SUBLANE, CONV_CH), F32), pltpu.VMEM((D_STATE, D_SSD), F32)],
        compiler_params=_cparams(("parallel", "arbitrary"), 48),
        name="ssd_prompt",
    )(proj, proj, proj, cw, cb, dtb, a, dsk, nw, e64, e128)


def _ssd_sample_kernel(xbc_ref, z_ref, dt_ref, cbuf_ref, st_in_ref, cw_ref, cb_ref, dtb_ref, a_ref,
                       dsk_ref, nw_ref, e64_ref, out_ref, st_out_ref, conv_ref,
                       xdt_t, dec_t, b_scr, c_scr, xs_scr, y_scr):
    s = pl.program_id(0)
    n = pl.num_programs(0)
    rows = N_SSD_HEADS * SSD_HEAD_DIM

    @pl.when(s == 0)
    def _():
        x_new = xbc_ref[...]
        acc = cb_ref[...] + cw_ref[CONV_W - 1:CONV_W, :] * x_new
        for k in range(CONV_W - 1):
            acc = acc + cw_ref[k:k + 1, :] * cbuf_ref[k]
        xbc = _silu(acc)
        for k in range(CONV_W - 2):
            conv_ref[k] = cbuf_ref[k + 1]
        conv_ref[CONV_W - 2] = x_new
        xs = xbc[:, :D_SSD]
        lane = lax.broadcasted_iota(jnp.int32, dt_ref.shape, 1)
        dt = jnp.where(lane < N_SSD_HEADS, _softplus(dt_ref[...] + dtb_ref[...]), 0.0)
        e64 = e64_ref[...]
        dt_e = _expand(dt, e64)
        da_e = _expand(dt * a_ref[...], e64)
        xs_scr[...] = xs
        xdt_t[...] = (xs * dt_e).T
        dec_t[...] = jnp.exp(da_e).T
        b_scr[...] = xbc[:, D_SSD:D_SSD + N_SSD_GROUPS * D_STATE]
        c_scr[...] = xbc[:, D_SSD + N_SSD_GROUPS * D_STATE:]

    st = st_in_ref[0].reshape(rows, D_STATE)
    lane = lax.broadcasted_iota(jnp.int32, (rows, xdt_t.shape[1]), 1)
    pick = lane == s
    xcol = jnp.sum(jnp.where(pick, xdt_t[...], 0.0), axis=1, keepdims=True)
    dcol = jnp.sum(jnp.where(pick, dec_t[...], 0.0), axis=1, keepdims=True)
    brow = b_scr[pl.ds(s, 1), :]
    crow = c_scr[pl.ds(s, 1), :]
    bfull = jnp.concatenate(
        [jnp.broadcast_to(brow[:, g * D_STATE:(g + 1) * D_STATE], (GROUP_W, D_STATE))
         for g in range(N_SSD_GROUPS)], axis=0)
    st_new = st * dcol + xcol * bfull
    st_out_ref[0] = st_new.reshape(N_SSD_HEADS, SSD_HEAD_DIM, D_STATE)
    st_b = st_new.astype(BF16)
    ys = []
    for g in range(N_SSD_GROUPS):
        cg = jnp.broadcast_to(crow[:, g * D_STATE:(g + 1) * D_STATE], (2 * SUBLANE, D_STATE)).astype(BF16)
        yg = _dot_nt(cg, st_b[g * GROUP_W:(g + 1) * GROUP_W, :])
        ys.append(yg[0:1, :])
    y_scr[pl.ds(s, 1), :] = jnp.concatenate(ys, axis=1)

    @pl.when(s == n - 1)
    def _():
        out_ref[...] = _gated_norm(y_scr[...], xs_scr[...], z_ref[...], dsk_ref[...],
                                   nw_ref[...]).astype(BF16)


def _ssd_sample(proj, conv_buf_t, state, cw, cb, dtb, a, dsk, nw, e64):
    nb = proj.shape[0]
    const2 = lambda shape: pl.BlockSpec(shape, lambda i: (0, 0))
    return pl.pallas_call(
        _ssd_sample_kernel,
        grid=(nb,),
        in_specs=[pl.BlockSpec((nb, CONV_CH), lambda i: (0, C_XBC // CONV_CH)),
                  pl.BlockSpec((nb, D_SSD), lambda i: (0, C_Z // D_SSD)),
                  pl.BlockSpec((nb, LANE), lambda i: (0, C_DT // LANE)),
                  pl.BlockSpec((CONV_W - 1, nb, CONV_CH), lambda i: (0, 0, 0)),
                  pl.BlockSpec((1, N_SSD_HEADS, SSD_HEAD_DIM, D_STATE), lambda i: (i, 0, 0, 0)),
                  const2((CONV_W, CONV_CH)), const2((1, CONV_CH)), const2((1, LANE)), const2((1, LANE)),
                  const2((1, D_SSD)), const2((1, D_SSD)), const2((LANE, D_SSD))],
        out_specs=[pl.BlockSpec((nb, D_SSD), lambda i: (0, 0)),
                   pl.BlockSpec((1, N_SSD_HEADS, SSD_HEAD_DIM, D_STATE), lambda i: (i, 0, 0, 0)),
                   pl.BlockSpec((CONV_W - 1, nb, CONV_CH), lambda i: (0, 0, 0))],
        out_shape=[jax.ShapeDtypeStruct((nb, D_SSD), BF16),
                   jax.ShapeDtypeStruct(state.shape, F32),
                   jax.ShapeDtypeStruct((CONV_W - 1, nb, CONV_CH), F32)],
        scratch_shapes=[pltpu.VMEM((D_SSD, nb), F32), pltpu.VMEM((D_SSD, nb), F32),
                        pltpu.VMEM((nb, N_SSD_GROUPS * D_STATE), F32),
                        pltpu.VMEM((nb, N_SSD_GROUPS * D_STATE), F32),
                        pltpu.VMEM((nb, D_SSD), F32), pltpu.VMEM((nb, D_SSD), F32)],
        compiler_params=_cparams(("arbitrary",), 48),
        name="ssd_sample",
    )(proj, proj, proj, conv_buf_t, state, cw, cb, dtb, a, dsk, nw, e64)


def _rms(x, w):
    return x * lax.rsqrt(jnp.mean(x * x, axis=1, keepdims=True) + NORM_EPS) * w


def _rope_fold(t):
    lane = lax.broadcasted_iota(jnp.int32, t.shape, 1)
    return jnp.where(lane < QK_ROPE_DIM, t + pltpu.roll(t, QK_ROPE_DIM, 1), 0.0)


def _mla_prep_kernel(cq_ref, ckv_ref, kr_ref, tab_ref, qnw_ref, kvnw_ref, wq_ref, wkv_ref,
                     q_ref, ckvn_ref, krope_ref, *kv_refs, with_kv):
    tab = tab_ref[...]
    qn = _rms(cq_ref[...], qnw_ref[...]).astype(BF16)
    qf = jnp.dot(qn, wq_ref[...], preferred_element_type=F32)
    for h in range(N_MLA_HEADS):
        base = h * QK_PAD
        q_ref[:, base:base + QK_NOPE_DIM] = qf[:, base:base + QK_NOPE_DIM].astype(BF16)
        q_ref[:, base + QK_NOPE_DIM:base + QK_PAD] = _rope_fold(
            qf[:, base + QK_NOPE_DIM:base + QK_PAD] * tab).astype(BF16)
    ckvn = _rms(ckv_ref[...], kvnw_ref[...])
    ckvn_ref[...] = ckvn
    kro = _rope_fold(kr_ref[...] * tab)
    krope_ref[...] = kro[:, :QK_ROPE_DIM]
    if with_kv:
        k_ref, v_ref = kv_refs
        kv = jnp.dot(ckvn.astype(BF16), wkv_ref[...], preferred_element_type=F32)
        kro_b = kro.astype(BF16)
        for h in range(N_MLA_HEADS):
            k_ref[:, h * QK_PAD:h * QK_PAD + QK_NOPE_DIM] = kv[:, h * LANE:(h + 1) * LANE].astype(BF16)
            k_ref[:, h * QK_PAD + QK_NOPE_DIM:(h + 1) * QK_PAD] = kro_b
        v_ref[...] = kv[:, N_MLA_HEADS * LANE:].astype(BF16)


def _mla_prep(proj, tab, qnw, kvnw, wq, wkv, tm, with_kv):
    m = proj.shape[0]
    npos = tab.shape[0] // tm
    const = lambda shape: pl.BlockSpec(shape, lambda i: (0, 0))
    out_specs = [pl.BlockSpec((tm, N_MLA_HEADS * QK_PAD), lambda i: (i, 0)),
                 pl.BlockSpec((tm, KV_LORA), lambda i: (i, 0)),
                 pl.BlockSpec((tm, QK_ROPE_DIM), lambda i: (i, 0))]
    out_shape = [jax.ShapeDtypeStruct((m, N_MLA_HEADS * QK_PAD), BF16),
                 jax.ShapeDtypeStruct((m, KV_LORA), F32),
                 jax.ShapeDtypeStruct((m, QK_ROPE_DIM), F32)]
    if with_kv:
        out_specs += [pl.BlockSpec((tm, N_MLA_HEADS * QK_PAD), lambda i: (i, 0)),
                      pl.BlockSpec((tm, N_MLA_HEADS * V_HEAD_DIM), lambda i: (i, 0))]
        out_shape += [jax.ShapeDtypeStruct((m, N_MLA_HEADS * QK_PAD), BF16),
                      jax.ShapeDtypeStruct((m, N_MLA_HEADS * V_HEAD_DIM), BF16)]
    return pl.pallas_call(
        functools.partial(_mla_prep_kernel, with_kv=with_kv),
        grid=(m // tm,),
        in_specs=[pl.BlockSpec((tm, Q_LORA), lambda i: (i, C_CQ // Q_LORA)),
                  pl.BlockSpec((tm, KV_LORA), lambda i: (i, C_CKV // KV_LORA)),
                  pl.BlockSpec((tm, LANE), lambda i: (i, C_KR // LANE)),
                  pl.BlockSpec((tm, LANE), lambda i: (i % npos, 0)),
                  const((1, Q_LORA)), const((1, KV_LORA)),
                  const((Q_LORA, N_MLA_HEADS * QK_PAD)),
                  const((KV_LORA, 2 * N_MLA_HEADS * V_HEAD_DIM))],
        out_specs=out_specs,
        out_shape=out_shape,
        compiler_params=_cparams(("parallel",), 56),
        name="mla_prep",
    )(proj, proj, proj, tab, qnw, kvnw, wq, wkv)


def _flash_kernel(q_ref, k_ref, v_ref, o_ref, *, tq):
    qi = pl.program_id(2)
    heads = range(FLASH_HEADS)
    qs = [q_ref[:, h * QK_PAD:(h + 1) * QK_PAD] for h in heads]

    def block(j, carry, diagonal):
        off = pl.multiple_of(j * tq, tq)
        out = []
        for h in heads:
            m, l, acc = carry[h]
            k = k_ref[pl.ds(off, tq), h * QK_PAD:(h + 1) * QK_PAD]
            v = v_ref[pl.ds(off, tq), h * V_HEAD_DIM:(h + 1) * V_HEAD_DIM]
            s = _dot_nt(qs[h], k) * (ATTN_SCALE * LOG2_E)
            if diagonal:
                row = lax.broadcasted_iota(jnp.int32, (tq, tq), 0)
                col = lax.broadcasted_iota(jnp.int32, (tq, tq), 1)
                s = jnp.where(col <= row, s, NEG_BIG)
            m_new = jnp.maximum(m, jnp.max(s, axis=1, keepdims=True))
            alpha = jnp.exp2(m - m_new)
            p = jnp.exp2(s - m_new)
            l = alpha * l + jnp.sum(p, axis=1, keepdims=True)
            acc = alpha * acc + jnp.dot(p.astype(BF16), v, preferred_element_type=F32)
            out.append((m_new, l, acc))
        return tuple(out)

    init = tuple((jnp.full((tq, 1), NEG_BIG, F32), jnp.zeros((tq, 1), F32),
                  jnp.zeros((tq, V_HEAD_DIM), F32)) for _ in heads)
    carry = lax.fori_loop(0, qi, functools.partial(block, diagonal=False), init)
    carry = block(qi, carry, True)
    for h in heads:
        _, l, acc = carry[h]
        o_ref[:, h * V_HEAD_DIM:(h + 1) * V_HEAD_DIM] = (acc / l).astype(BF16)


def _flash(q, k, v, b, t, tq):
    nq = t // tq
    return pl.pallas_call(
        functools.partial(_flash_kernel, tq=tq),
        grid=(b, N_MLA_HEADS // FLASH_HEADS, nq),
        in_specs=[pl.BlockSpec((tq, FLASH_HEADS * QK_PAD), lambda i, h, j: (i * nq + j, h)),
                  pl.BlockSpec((t, FLASH_HEADS * QK_PAD), lambda i, h, j: (i, h)),
                  pl.BlockSpec((t, FLASH_HEADS * V_HEAD_DIM), lambda i, h, j: (i, h))],
        out_specs=pl.BlockSpec((tq, FLASH_HEADS * V_HEAD_DIM), lambda i, h, j: (i * nq + j, h)),
        out_shape=jax.ShapeDtypeStruct((b * t, N_MLA_HEADS * V_HEAD_DIM), BF16),
        compiler_params=_cparams(("parallel", "parallel", "arbitrary"), 48),
        name="flash_prompt",
    )(q, k, v)


def _qlat_kernel(qn_ref, qr_ref, w_ref, o_ref):
    o_ref[:, :KV_LORA] = jnp.dot(qn_ref[...], w_ref[0], preferred_element_type=F32).astype(BF16)
    o_ref[:, KV_LORA:] = qr_ref[...]


def _q_latent(q, w_uk_t):
    nb = q.shape[0]
    return pl.pallas_call(
        _qlat_kernel,
        grid=(N_MLA_HEADS,),
        in_specs=[pl.BlockSpec((nb, QK_NOPE_DIM), lambda h: (0, 2 * h)),
                  pl.BlockSpec((nb, LANE), lambda h: (0, 2 * h + 1)),
                  pl.BlockSpec((1, QK_NOPE_DIM, KV_LORA), lambda h: (h, 0, 0))],
        out_specs=pl.BlockSpec((nb, QLAT_W), lambda h: (0, h)),
        out_shape=jax.ShapeDtypeStruct((nb, N_MLA_HEADS * QLAT_W), BF16),
        compiler_params=_cparams(("parallel",), 32),
        name="q_latent",
    )(q, q, w_uk_t)


def _vup_kernel(o_ref, w_ref, out_ref):
    out_ref[...] = jnp.dot(o_ref[...], w_ref[0], preferred_element_type=F32).astype(BF16)


def _v_up(o_lat, w_uv_h):
    nb = o_lat.shape[0]
    return pl.pallas_call(
        _vup_kernel,
        grid=(N_MLA_HEADS,),
        in_specs=[pl.BlockSpec((nb, KV_LORA), lambda h: (0, h)),
                  pl.BlockSpec((1, KV_LORA, V_HEAD_DIM), lambda h: (h, 0, 0))],
        out_specs=pl.BlockSpec((nb, V_HEAD_DIM), lambda h: (0, h)),
        out_shape=jax.ShapeDtypeStruct((nb, N_MLA_HEADS * V_HEAD_DIM), BF16),
        compiler_params=_cparams(("parallel",), 32),
        name="v_up",
    )(o_lat, w_uv_h)


def _page_copies(pt_ref, ckv_hbm, kr_hbm, cbuf, rbuf, sems, first_page, slot, pages):
    copies = []
    for j in range(pages):
        pg = pt_ref[first_page + j]
        copies.append(pltpu.make_async_copy(ckv_hbm.at[0, pg], cbuf.at[slot, j], sems.at[0, slot]))
        copies.append(pltpu.make_async_copy(kr_hbm.at[0, pg], rbuf.at[slot, j], sems.at[1, slot]))
    return copies


def _decode_kernel(pt_ref, q_ref, cnew_ref, rnew_ref, ckv_hbm, kr_hbm, o_ref,
                   cbuf, rbuf, sems, m_scr, l_scr, acc_scr, *, pages):
    s = pl.program_id(1)
    ns = pl.num_programs(1)
    step = pl.program_id(0) * ns + s
    last_step = pl.num_programs(0) * ns - 1
    fetch = functools.partial(_page_copies, pt_ref, ckv_hbm, kr_hbm, cbuf, rbuf, sems, pages=pages)
    q = q_ref[0]
    ql = q[:, :KV_LORA]
    qr = q[:, KV_LORA:KV_LORA + QK_ROPE_DIM]
    group = math.gcd(pages, DECODE_GROUP)

    @pl.when(step == 0)
    def _():
        for cp in fetch(0, 0):
            cp.start()

    @pl.when(s == 0)
    def _():
        m_scr[...] = jnp.full_like(m_scr, NEG_BIG)
        l_scr[...] = jnp.zeros_like(l_scr)
        acc_scr[...] = jnp.zeros_like(acc_scr)

    def attend(slot):
        for cp in fetch(jnp.minimum(step + 1, last_step) * pages, 1 - slot):
            cp.start()
        for cp in fetch(step * pages, slot):
            cp.wait()
        scores, latents = [], []
        for g in range(pages // group):
            cb = cbuf[slot, g * group:(g + 1) * group].reshape(group * PAGE_SIZE, KV_LORA).astype(BF16)
            rb = jnp.concatenate([rbuf[slot, j].astype(BF16) for j in range(g * group, (g + 1) * group)],
                                 axis=1)
            scores.append((_dot_nt(ql, cb) + jnp.dot(qr, rb, preferred_element_type=F32)) * ATTN_SCALE)
            latents.append(cb)
        sc = jnp.concatenate(scores, axis=1)
        m_old = m_scr[...]
        m_new = jnp.maximum(m_old, jnp.max(sc, axis=1, keepdims=True))
        alpha = jnp.exp(m_old - m_new)
        p = jnp.exp(sc - m_new)
        l_scr[...] = alpha * l_scr[...] + jnp.sum(p, axis=1, keepdims=True)
        pb = p.astype(BF16)
        keys = group * PAGE_SIZE
        pv = jnp.dot(pb[:, :keys], latents[0], preferred_element_type=F32)
        for g in range(1, pages // group):
            pv = pv + jnp.dot(pb[:, g * keys:(g + 1) * keys], latents[g], preferred_element_type=F32)
        acc_scr[...] = alpha * acc_scr[...] + pv
        m_scr[...] = m_new

    for slot in range(2):
        @pl.when(lax.rem(step, 2) == slot)
        def _(slot=slot):
            attend(slot)

    @pl.when(step == last_step)
    def _():
        for cp in fetch(last_step * pages, 1 - lax.rem(step, 2)):
            cp.wait()

    @pl.when(s == pl.num_programs(1) - 1)
    def _():
        cn = cnew_ref[0].astype(BF16).astype(F32)
        rn = rnew_ref[0].astype(BF16).astype(F32)
        s_new = (jnp.sum(ql.astype(F32) * cn, axis=1, keepdims=True)
                 + jnp.sum(qr.astype(F32) * rn, axis=1, keepdims=True)) * ATTN_SCALE
        m_old = m_scr[...]
        m_fin = jnp.maximum(m_old, s_new)
        alpha = jnp.exp(m_old - m_fin)
        p_new = jnp.exp(s_new - m_fin)
        l_fin = alpha * l_scr[...] + p_new
        acc = alpha * acc_scr[...] + p_new.astype(BF16).astype(F32) * cn
        o_ref[0] = (acc / l_fin).astype(BF16)


def _decode(q_cat, ckv_new, kr_new, cache_ckv, cache_krope_t, page_table, pages):
    nb, n_pages = page_table.shape
    steps = n_pages // pages

    grid_spec = pltpu.PrefetchScalarGridSpec(
        num_scalar_prefetch=1,
        grid=(nb, steps),
        in_specs=[pl.BlockSpec((1, N_MLA_HEADS, QLAT_W), lambda b, s, pt: (b, 0, 0)),
                  pl.BlockSpec((1, 1, KV_LORA), lambda b, s, pt: (b, 0, 0)),
                  pl.BlockSpec((1, 1, QK_ROPE_DIM), lambda b, s, pt: (b, 0, 0)),
                  pl.BlockSpec(memory_space=pl.ANY), pl.BlockSpec(memory_space=pl.ANY)],
        out_specs=pl.BlockSpec((1, N_MLA_HEADS, KV_LORA), lambda b, s, pt: (b, 0, 0)),
        scratch_shapes=[pltpu.VMEM((2, pages, PAGE_SIZE, KV_LORA), F32),
                        pltpu.VMEM((2, pages, QK_ROPE_DIM, PAGE_SIZE), F32),
                        pltpu.SemaphoreType.DMA((2, 2)),
                        pltpu.VMEM((N_MLA_HEADS, 1), F32), pltpu.VMEM((N_MLA_HEADS, 1), F32),
                        pltpu.VMEM((N_MLA_HEADS, KV_LORA), F32)],
    )
    return pl.pallas_call(
        functools.partial(_decode_kernel, pages=pages),
        grid_spec=grid_spec,
        out_shape=jax.ShapeDtypeStruct((nb, N_MLA_HEADS, KV_LORA), BF16),
        compiler_params=_cparams(("arbitrary", "arbitrary"), 48),
        name="decode_attn",
    )(page_table.reshape(-1), q_cat, ckv_new, kr_new, cache_ckv, cache_krope_t)


def _layer_norm(x, g, b):
    mu = jnp.mean(x, axis=1, keepdims=True)
    xc = x - mu
    var = jnp.mean(xc * xc, axis=1, keepdims=True)
    return xc * lax.rsqrt(var + NORM_EPS) * g + b


def _outproj_kernel(*refs, tn, aliased):
    if aliased:
        refs = refs[1:]
    ssd_ref, mla_ref, wt_ref, wb_ref, x_ref, g_ref, b_ref, o_ref = refs
    j = pl.program_id(1)
    mix = (jnp.dot(ssd_ref[...], wt_ref[...], preferred_element_type=F32)
           + jnp.dot(mla_ref[...], wb_ref[...], preferred_element_type=F32))
    val = DEEPNORM_ALPHA * x_ref[...] + mix
    nj = D_MODEL // tn
    for jj in range(nj):
        @pl.when(j == jj)
        def _(jj=jj):
            o_ref[:, jj * tn:(jj + 1) * tn] = val

    @pl.when(j == nj - 1)
    def _():
        o_ref[...] = _layer_norm(o_ref[...], g_ref[...], b_ref[...])


def _outproj_ln(ssd, mla, w_out_b, x, g, b, tm, total_rows, row_block0, buf=None):
    m = x.shape[0]
    tn = 512
    aliased = buf is not None
    in_specs = [pl.BlockSpec((tm, D_SSD), lambda i, j: (i, 0)),
                pl.BlockSpec((tm, D_SSD), lambda i, j: (i, 0)),
                pl.BlockSpec((D_SSD, tn), lambda i, j: (0, j)),
                pl.BlockSpec((D_SSD, tn), lambda i, j: (1, j)),
                pl.BlockSpec((tm, tn), lambda i, j: (i, j)),
                pl.BlockSpec((1, D_MODEL), lambda i, j: (0, 0)),
                pl.BlockSpec((1, D_MODEL), lambda i, j: (0, 0))]
    args = [ssd, mla, w_out_b, w_out_b, x, g, b]
    if aliased:
        in_specs = [pl.BlockSpec(memory_space=pl.ANY)] + in_specs
        args = [buf] + args
    return pl.pallas_call(
        functools.partial(_outproj_kernel, tn=tn, aliased=aliased),
        grid=(m // tm, D_MODEL // tn),
        in_specs=in_specs,
        out_specs=pl.BlockSpec((tm, D_MODEL), lambda i, j: (i + row_block0, 0)),
        out_shape=jax.ShapeDtypeStruct((total_rows, D_MODEL), F32),
        input_output_aliases={0: 0} if aliased else {},
        compiler_params=_cparams(("parallel", "arbitrary"), 56),
        name="out_proj_ln",
    )(*args)


def _router_kernel(h_ref, wh_ref, wm_ref, bias_ref, ids_ref, gates_ref):
    hh, hm, _ = _split3(h_ref[...])
    wh = wh_ref[...]
    d = functools.partial(jnp.dot, preferred_element_type=F32)
    lg = d(hh, wh) + (d(hh, wm_ref[...]) + d(hm, wh)) + bias_ref[...]
    lane = lax.broadcasted_iota(jnp.int32, lg.shape, 1)
    big = jnp.int32(1 << 20)
    gmask = lane < N_EXPERT_GROUPS
    gl = jnp.where(gmask, lg, -jnp.inf)
    ge = jnp.exp(gl - jnp.max(gl, axis=1, keepdims=True))
    gp = ge / jnp.sum(ge, axis=1, keepdims=True)
    g_val = jnp.max(gp, axis=1, keepdims=True)
    g_idx = jnp.min(jnp.where(gmask & (gp == g_val), lane, big), axis=1, keepdims=True)
    eid = lane - N_EXPERT_GROUPS
    emask = (eid >= 0) & (eid < N_EXPERTS) & (lax.shift_right_arithmetic(eid, 3) == g_idx)
    el = jnp.where(emask, lg, -jnp.inf)
    ee = jnp.exp(el - jnp.max(el, axis=1, keepdims=True))
    ep = jnp.where(emask, ee / jnp.sum(ee, axis=1, keepdims=True), -1.0)
    v1 = jnp.max(ep, axis=1, keepdims=True)
    i1 = jnp.min(jnp.where(emask & (ep == v1), lane, big), axis=1, keepdims=True)
    ep2 = jnp.where(lane == i1, -1.0, ep)
    v2 = jnp.max(ep2, axis=1, keepdims=True)
    i2 = jnp.min(jnp.where(emask & (lane != i1) & (ep2 == v2), lane, big), axis=1, keepdims=True)
    tot = v1 + v2
    ids_ref[...] = jnp.where(lane == 0, i1 - N_EXPERT_GROUPS,
                             jnp.where(lane == 1, i2 - N_EXPERT_GROUPS, 0))
    gates_ref[...] = jnp.where(lane == 0, g_val * (v1 / tot), jnp.where(lane == 1, g_val * (v2 / tot), 0.0))


def _router(h, wh, wm, bias, tm):
    t = h.shape[0]
    const = lambda shape: pl.BlockSpec(shape, lambda i: (0, 0))
    return pl.pallas_call(
        _router_kernel,
        grid=(t // tm,),
        in_specs=[pl.BlockSpec((tm, D_MODEL), lambda i: (i, 0)),
                  const((D_MODEL, LANE)), const((D_MODEL, LANE)), const((1, LANE))],
        out_specs=[pl.BlockSpec((tm, LANE), lambda i: (i, 0)), pl.BlockSpec((tm, LANE), lambda i: (i, 0))],
        out_shape=[jax.ShapeDtypeStruct((t, LANE), jnp.int32), jax.ShapeDtypeStruct((t, LANE), F32)],
        compiler_params=_cparams(("parallel",), 48),
        name="router",
    )(h, wh, wm, bias)


def _row_copy(src, src_row, dst, dst_row, sem):
    return pltpu.make_async_copy(src.at[pl.ds(src_row, 1)], dst.at[pl.ds(dst_row, 1)], sem)


def _for_rows(count, fn):
    full = count // ROW_UNROLL

    def body_unrolled(c, carry):
        for u in range(ROW_UNROLL):
            fn(c * ROW_UNROLL + u)
        return carry

    def body_one(r, carry):
        fn(r)
        return carry

    lax.fori_loop(0, full, body_unrolled, 0)
    lax.fori_loop(full * ROW_UNROLL, count, body_one, 0)


def _moe_kernel(blk_e, blk_start, blk_n, n_live, tok, dst, h_hbm, wg_ref, wu_ref, wd_ref, y_hbm,
                xg, xb, acc, sem_in, sem_out):
    i = pl.program_id(0)
    n = pl.program_id(1)
    last_i = pl.num_programs(0) - 1
    last_n = pl.num_programs(1) - 1
    nrows = blk_n[i]

    def start_gather(blk):
        base = blk_start[blk]
        _for_rows(blk_n[blk], lambda r: _row_copy(h_hbm, tok[base + r], xg, r, sem_in).start())

    def start_scatter(blk):
        base = blk_start[blk]
        _for_rows(blk_n[blk], lambda r: _row_copy(acc, r, y_hbm, dst[base + r], sem_out).start())

    def wait_rows(src, dst_ref, sem, count):
        _for_rows(count, lambda r: _row_copy(src, 0, dst_ref, 0, sem).wait())

    @pl.when((i == 0) & (n == 0))
    def _():
        xg[...] = jnp.zeros_like(xg)
        acc[...] = jnp.zeros_like(acc)
        start_gather(0)

    @pl.when(n == 0)
    def _():
        wait_rows(h_hbm, xg, sem_in, nrows)

        @pl.when(i > 0)
        def _():
            wait_rows(acc, y_hbm, sem_out, blk_n[jnp.maximum(i - 1, 0)])

    for k in range(1, MOE_BLK // MOE_SUB + 1):
        rows = k * MOE_SUB

        @pl.when((nrows > rows - MOE_SUB) & (nrows <= rows))
        def _(rows=rows):
            @pl.when(n == 0)
            def _():
                xb[0:rows, :] = xg[0:rows, :].astype(BF16)

            x = xb[0:rows, :]
            hid = (_silu(jnp.dot(x, wg_ref[...].astype(BF16), preferred_element_type=F32))
                   * jnp.dot(x, wu_ref[...].astype(BF16), preferred_element_type=F32)).astype(BF16)
            wd = wd_ref[...].astype(BF16)
            for c in range(D_MODEL // MOE_DOWN_TN):
                cs = slice(c * MOE_DOWN_TN, (c + 1) * MOE_DOWN_TN)
                part = jnp.dot(hid, wd[:, cs], preferred_element_type=F32)
                acc[0:rows, cs] = jnp.where(n > 0, acc[0:rows, cs], 0.0) + part

    @pl.when((n == 0) & (i < last_i))
    def _():
        start_gather(i + 1)

    @pl.when(n == last_n)
    def _():
        start_scatter(i)

        @pl.when(i == last_i)
        def _():
            wait_rows(acc, y_hbm, sem_out, nrows)


def _moe(h_all, blk_e, blk_start, blk_n, n_live, tok, dst, w_gate_e, w_up_e, w_down_e, nb_max):
    t = h_all.shape[0]
    nt = D_EXPERT // MOE_TN

    def live_n(i, n, nl):
        return jnp.where(i < nl[0], n, nt - 1)

    grid_spec = pltpu.PrefetchScalarGridSpec(
        num_scalar_prefetch=6,
        grid=(nb_max, nt),
        in_specs=[pl.BlockSpec(memory_space=pl.ANY),
                  pl.BlockSpec((None, None, D_MODEL, MOE_TN),
                               lambda i, n, be, bs, bn, nl, tk, ds: (0, be[i], 0, live_n(i, n, nl))),
                  pl.BlockSpec((None, None, D_MODEL, MOE_TN),
                               lambda i, n, be, bs, bn, nl, tk, ds: (0, be[i], 0, live_n(i, n, nl))),
                  pl.BlockSpec((None, None, MOE_TN, D_MODEL),
                               lambda i, n, be, bs, bn, nl, tk, ds: (0, be[i], live_n(i, n, nl), 0))],
        out_specs=pl.BlockSpec(memory_space=pl.ANY),
        scratch_shapes=[pltpu.VMEM((MOE_BLK, D_MODEL), F32), pltpu.VMEM((MOE_BLK, D_MODEL), BF16),
                        pltpu.VMEM((MOE_BLK, D_MODEL), F32),
                        pltpu.SemaphoreType.DMA(()), pltpu.SemaphoreType.DMA(())],
    )
    return pl.pallas_call(
        _moe_kernel,
        grid_spec=grid_spec,
        out_shape=jax.ShapeDtypeStruct((TOP_K * t, D_MODEL), F32),
        compiler_params=_cparams(("arbitrary", "arbitrary"), 56),
        name="moe_experts",
    )(blk_e, blk_start, blk_n, n_live, tok, dst, h_all, w_gate_e, w_up_e, w_down_e)


def _dispatch_plan(ids, nb_max):
    t = ids.shape[0]
    n_assign = TOP_K * t
    e_flat = ids.reshape(-1)
    order = jnp.argsort(e_flat).astype(jnp.int32)
    counts = jnp.bincount(e_flat, length=N_EXPERTS).astype(jnp.int32)
    start = jnp.cumsum(counts) - counts
    nblk = (counts + MOE_BLK - 1) // MOE_BLK
    blk_end = jnp.cumsum(nblk)
    n_live = blk_end[-1]
    bi = jnp.arange(nb_max, dtype=jnp.int32)
    live = bi < n_live
    be = jnp.minimum(jnp.searchsorted(blk_end, bi, side='right'), N_EXPERTS - 1).astype(jnp.int32)
    last_e = jnp.max(jnp.where(counts > 0, jnp.arange(N_EXPERTS, dtype=jnp.int32), 0))
    be = jnp.where(live, be, last_e)
    k = bi - (blk_end[be] - nblk[be])
    bstart = jnp.where(live, start[be] + k * MOE_BLK, 0).astype(jnp.int32)
    bn = jnp.where(live, jnp.clip(counts[be] - k * MOE_BLK, 0, MOE_BLK), 0).astype(jnp.int32)
    tok = order // TOP_K
    dst = (order % TOP_K) * t + tok
    return be, bstart, bn, n_live.reshape(1).astype(jnp.int32), tok, dst


def _ln2_kernel(h_ref, y0_ref, y1_ref, gt_ref, g_ref, b_ref, op_ref, os_ref, *, n_prompt_blocks):
    i = pl.program_id(0)
    gt = gt_ref[...]
    ffn = gt[:, 0:1] * y0_ref[...] + gt[:, 1:2] * y1_ref[...]
    out = _layer_norm(DEEPNORM_ALPHA * h_ref[...] + ffn, g_ref[...], b_ref[...])

    @pl.when(i < n_prompt_blocks)
    def _():
        op_ref[...] = out

    @pl.when(i >= n_prompt_blocks)
    def _():
        os_ref[...] = out


def _ln2(h_all, y2, gates, g, b, n_prompt, tm):
    t = h_all.shape[0]
    nblk = t // tm
    npb = n_prompt // tm
    const = lambda shape: pl.BlockSpec(shape, lambda i: (0, 0))
    return pl.pallas_call(
        functools.partial(_ln2_kernel, n_prompt_blocks=npb),
        grid=(nblk,),
        in_specs=[pl.BlockSpec((tm, D_MODEL), lambda i: (i, 0)),
                  pl.BlockSpec((tm, D_MODEL), lambda i: (i, 0)),
                  pl.BlockSpec((tm, D_MODEL), lambda i: (i + nblk, 0)),
                  pl.BlockSpec((tm, LANE), lambda i: (i, 0)),
                  const((1, D_MODEL)), const((1, D_MODEL))],
        out_specs=[pl.BlockSpec((tm, D_MODEL), lambda i: (jnp.minimum(i, npb - 1), 0)),
                   pl.BlockSpec((tm, D_MODEL), lambda i: (jnp.maximum(i - npb, 0), 0))],
        out_shape=[jax.ShapeDtypeStruct((n_prompt, D_MODEL), F32),
                   jax.ShapeDtypeStruct((t - n_prompt, D_MODEL), F32)],
        compiler_params=_cparams(("arbitrary",), 48),
        name="ffn_ln",
    )(h_all, y2, y2, gates, g, b)


def _rot_cols(w):
    half = w.shape[-1] // 2
    return jnp.concatenate([-w[..., half:], w[..., :half]], axis=-1)


def _rope_table(pos):
    half = QK_ROPE_DIM // 2
    inv_freq = ROPE_THETA ** (-jnp.arange(half, dtype=F32) / half)
    ang = pos.astype(F32)[:, None] * inv_freq
    cos, sin = jnp.cos(ang), jnp.sin(ang)
    return jnp.concatenate([cos, cos, sin, sin], axis=1)


def _expander(width):
    k = lax.broadcasted_iota(jnp.int32, (LANE, N_SSD_HEADS * width), 0)
    c = lax.broadcasted_iota(jnp.int32, (LANE, N_SSD_HEADS * width), 1)
    return (k == c // width).astype(BF16)


def _row_tile(m, pref):
    while m % pref:
        pref //= 2
    return pref


def kernel(x_prompt, x_sample, cache_ckv, cache_krope, state_ssm, state_conv, page_table, w_in, conv_w, conv_b, dt_bias, a_log, d_skip, ssd_norm_w, q_norm_w, w_uq, kv_norm_w, w_uk, w_uv, w_out, ln1_g, ln1_b, w_router_group, b_router_group, w_router_expert, b_router_expert, w_gate_e, w_up_e, w_down_e, ln2_g, ln2_b):
    b, t, _ = x_prompt.shape
    nb, dec_t, _ = x_sample.shape
    assert dec_t == 1 and w_in.shape[0] == 1
    n_pages = page_table.shape[1]
    past = n_pages * PAGE_SIZE
    n_prompt = b * t
    n_tok = n_prompt + nb

    wt = jnp.swapaxes(w_in[0], 0, 1)
    s0 = D_SSD
    s1 = s0 + CONV_CH
    s2 = s1 + N_SSD_HEADS
    s3 = s2 + Q_LORA
    s4 = s3 + KV_LORA
    w_kr_t = wt[s4:]
    half = QK_ROPE_DIM // 2
    w_all_t = jnp.concatenate(
        [wt[s0:s1], wt[s2:s3], wt[:s0], wt[s3:s4], w_kr_t, -w_kr_t[half:], w_kr_t[:half], wt[s1:s2],
         jnp.zeros((LANE - N_SSD_HEADS, D_MODEL), F32)], axis=0).astype(BF16)
    pad_heads = lambda v: jnp.pad(v, (0, LANE - N_SSD_HEADS)).reshape(1, LANE)
    dtb = pad_heads(dt_bias[0])
    a_neg = pad_heads(-jnp.exp(a_log[0].astype(F32)))
    dsk = jnp.repeat(d_skip[0], SSD_HEAD_DIM).reshape(1, D_SSD)
    nw = ssd_norm_w[0].reshape(1, D_SSD)
    cw = conv_w[0]
    cb = conv_b[0].reshape(1, CONV_CH)
    e64 = _expander(SSD_HEAD_DIM)
    e128 = _expander(LANE)
    wq = w_uq[0]
    wq_r = wq[..., QK_NOPE_DIM:]
    wq_all = jnp.concatenate([wq[..., :QK_NOPE_DIM], wq_r, _rot_cols(wq_r)], axis=-1)
    wq_all = wq_all.reshape(Q_LORA, N_MLA_HEADS * QK_PAD).astype(BF16)
    wkv = jnp.concatenate([w_uk[0].reshape(KV_LORA, -1), w_uv[0].reshape(KV_LORA, -1)], axis=1).astype(BF16)
    w_uk_t = jnp.transpose(w_uk[0], (1, 2, 0)).astype(BF16)
    w_uv_h = jnp.transpose(w_uv[0], (1, 0, 2)).astype(BF16)
    w_out_b = w_out[0].astype(BF16)
    qnw = q_norm_w[0].reshape(1, Q_LORA)
    kvnw = kv_norm_w[0].reshape(1, KV_LORA)
    g1, b1 = ln1_g[0].reshape(1, D_MODEL), ln1_b[0].reshape(1, D_MODEL)
    g2, b2 = ln2_g[0].reshape(1, D_MODEL), ln2_b[0].reshape(1, D_MODEL)
    w_r = jnp.concatenate([w_router_group[0], w_router_expert[0],
                           jnp.zeros((D_MODEL, LANE - N_EXPERT_GROUPS - N_EXPERTS), F32)], axis=1)
    w_r_hi = w_r.astype(BF16)
    w_r_mid = (w_r - w_r_hi.astype(F32)).astype(BF16)
    b_r = jnp.concatenate([b_router_group[0], b_router_expert[0],
                           jnp.zeros((LANE - N_EXPERT_GROUPS - N_EXPERTS,), F32)]).reshape(1, LANE)

    xp = x_prompt.reshape(n_prompt, D_MODEL)
    xsm = x_sample.reshape(nb, D_MODEL)

    proj_p = _matmul(xp, w_all_t, _row_tile(n_prompt, 512), 768)
    ssd_p, ssm_p, conv_p = _ssd_prompt(proj_p, b, t, cw, cb, dtb, a_neg, dsk, nw, e64, e128)
    tm_p = _row_tile(t, 256)
    tab_p = _rope_table(jnp.arange(t))
    q_p, ckv_p, kr_p, k_p, v_p = _mla_prep(proj_p, tab_p, qnw, kvnw, wq_all, wkv, tm_p, True)
    mla_p = _flash(q_p, k_p, v_p, b, t, _row_tile(t, 512))
    tm_o = _row_tile(n_prompt, 512)
    h_all = _outproj_ln(ssd_p, mla_p, w_out_b, xp, g1, b1, tm_o, n_tok, 0,
                        buf=jnp.zeros((n_tok, D_MODEL), F32))

    proj_s = _matmul(xsm, w_all_t, nb, 768)
    conv_buf_t = jnp.transpose(state_conv[0], (1, 0, 2))
    ssd_s, ssm_s, conv_s_t = _ssd_sample(proj_s, conv_buf_t, state_ssm[0], cw, cb, dtb, a_neg, dsk, nw, e64)
    tab_s = _rope_table(jnp.full((nb,), past, jnp.int32))
    q_s, ckv_s, kr_s = _mla_prep(proj_s, tab_s, qnw, kvnw, wq_all, wkv, nb, False)
    q_cat = _q_latent(q_s, w_uk_t).reshape(nb, N_MLA_HEADS, QLAT_W)
    pages = math.gcd(n_pages, DECODE_PAGES)
    o_lat = _decode(q_cat, ckv_s.reshape(nb, 1, KV_LORA), kr_s.reshape(nb, 1, QK_ROPE_DIM),
                    cache_ckv, jnp.swapaxes(cache_krope, 2, 3), page_table, pages)
    mla_s = _v_up(o_lat.reshape(nb, N_MLA_HEADS * KV_LORA), w_uv_h)
    h_all = _outproj_ln(ssd_s, mla_s, w_out_b, xsm, g1, b1, nb, n_tok, n_prompt // nb, buf=h_all)

    tm_r = _row_tile(math.gcd(n_prompt, nb), 128)
    ids, gates = _router(h_all, w_r_hi, w_r_mid, b_r, _row_tile(n_tok, 640))
    nb_max = (TOP_K * n_tok) // MOE_BLK + N_EXPERTS
    be, bstart, bn, n_live, tok, dst = _dispatch_plan(ids[:, :TOP_K], nb_max)
    y2 = _moe(h_all, be, bstart, bn, n_live, tok, dst, w_gate_e, w_up_e, w_down_e, nb_max)
    y_p, y_s = _ln2(h_all, y2, gates, g2, b2, n_prompt, tm_r)

    return (y_p.reshape(b, t, D_MODEL), y_s.reshape(nb, 1, D_MODEL),
            ckv_p.reshape(1, b, t, KV_LORA), kr_p.reshape(1, b, t, QK_ROPE_DIM),
            ssm_p[None], conv_p[None],
            ckv_s.reshape(1, nb, 1, KV_LORA), kr_s.reshape(1, nb, 1, QK_ROPE_DIM),
            ssm_s[None], jnp.transpose(conv_s_t, (1, 0, 2))[None])
```
